```python
import jax, jax.numpy as jnp
from jax import lax
import numpy as np

D_MODEL = 2048
BATCH = 16
SEQ = 2048
DEPTH = 4

A_HEADS = D_MODEL // 256
A_KEY_DIM = 128
A_VAL_DIM = 128
A_KEY_WIDTH = A_HEADS * A_KEY_DIM
A_VAL_WIDTH = A_HEADS * A_VAL_DIM
A_CHUNK = 32
B_GROUPS = D_MODEL // 256
B_GROUP_DIM = 128
B_WIDTH = B_GROUPS * B_GROUP_DIM
B_CHUNK = 128
FF_HIDDEN = 4 * D_MODEL
N_MOD = 6
EPS = 1e-6
MIN_FORGET = 1e-20
IN_SIZES = (A_KEY_WIDTH, A_KEY_WIDTH, A_VAL_WIDTH, A_VAL_WIDTH, B_WIDTH, B_WIDTH, D_MODEL, D_MODEL)
IN_WIDTH = 2 * A_KEY_WIDTH + 2 * A_VAL_WIDTH + 2 * B_WIDTH + 2 * D_MODEL

kernel_name = "hgrn2_gmlp_gated_hybrid_adaln"


def _split_points():
    pts, acc = [], 0
    for s in IN_SIZES[:-1]:
        acc += s
        pts.append(acc)
    return pts


def rms_norm(x, g):
    xf = x.astype(jnp.float32)
    y = xf * lax.rsqrt(jnp.mean(xf * xf, axis=-1, keepdims=True) + EPS)
    return (y * g.astype(jnp.float32)).astype(x.dtype)


def layer_norm(x, g, b):
    xf = x.astype(jnp.float32)
    mu = jnp.mean(xf, axis=-1, keepdims=True)
    var = jnp.mean(jnp.square(xf - mu), axis=-1, keepdims=True)
    y = (xf - mu) * lax.rsqrt(var + EPS)
    return (y * g.astype(jnp.float32) + b.astype(jnp.float32)).astype(x.dtype)


def modulate(h, shift, scale):
    return h * (1.0 + scale[:, None, :]) + shift[:, None, :]


def hgrn2_chunkwise(q, k, v, log_f):
    Bn, T, H, K = q.shape
    V = v.shape[-1]
    n = T // A_CHUNK

    def to_chunks(a):
        return a.reshape(Bn, n, A_CHUNK, H, a.shape[-1]).transpose(1, 0, 3, 2, 4)

    causal = jnp.tril(jnp.ones((A_CHUNK, A_CHUNK), dtype=bool))[:, :, None]
    causal_f = causal.astype(jnp.float32)

    def step(S, inp):
        qc, kc, vc, lc = inp
        b = jnp.cumsum(lc, axis=2)
        o_inter = jnp.einsum('bhtk,bhkv->bhtv', qc * jnp.exp(b), S)
        diff = b[:, :, :, None, :] - b[:, :, None, :, :]
        decay = jnp.exp(jnp.where(causal, diff, 0.0)) * causal_f
        scores = jnp.einsum('bhtk,bhsk,bhtsk->bhts', qc, kc, decay)
        o = o_inter + jnp.einsum('bhts,bhsv->bhtv', scores, vc)
        b_last = b[:, :, -1:, :]
        S_new = jnp.exp(b_last[:, :, 0, :, None]) * S + jnp.einsum(
            'bhsk,bhsv->bhkv', kc * jnp.exp(b_last - b), vc)
        return S_new, o

    S0 = jnp.zeros((Bn, H, K, V), dtype=jnp.float32)
    _, o = lax.scan(step, S0, (to_chunks(q), to_chunks(k), to_chunks(v), to_chunks(log_f)))
    return o.transpose(1, 0, 3, 2, 4).reshape(Bn, T, H, V)


def hgrn2_mixer(q_raw, f_raw, i_raw, g_raw, lb, norm_g):
    Bn, T, _ = q_raw.shape
    f32 = jnp.float32
    q = jax.nn.silu(q_raw.astype(f32)).reshape(Bn, T, A_HEADS, A_KEY_DIM)
    fl = f_raw.astype(f32)
    lbf = lb.astype(f32)
    f = lbf + (1.0 - lbf) * jax.nn.sigmoid(fl)
    log_f = jnp.log(jnp.maximum(f, MIN_FORGET))
    k = (1.0 - lbf) * jax.nn.sigmoid(-fl)
    log_f = log_f.reshape(Bn, T, A_HEADS, A_KEY_DIM)
    k = k.reshape(Bn, T, A_HEADS, A_KEY_DIM)
    v = i_raw.astype(f32).reshape(Bn, T, A_HEADS, A_VAL_DIM)
    o = hgrn2_chunkwise(q, k, v, log_f)
    o = rms_norm(o, norm_g).reshape(Bn, T, A_VAL_WIDTH)
    o = o * jax.nn.silu(g_raw.astype(f32))
    return o.astype(q_raw.dtype)


def spatial_gating(u_raw, v_raw, ln_g, ln_b, w_s, b_s):
    Bn, T, _ = v_raw.shape
    n = T // B_CHUNK
    u = jax.nn.gelu(u_raw, approximate=False)
    v = layer_norm(jax.nn.gelu(v_raw, approximate=False), ln_g, ln_b)
    v = v.reshape(Bn, n, B_CHUNK, B_GROUPS, B_GROUP_DIM)
    w = w_s * jnp.tril(jnp.ones((B_CHUNK, B_CHUNK), dtype=w_s.dtype))[None]
    s = jnp.einsum('gts,bnsgd->bntgd', w, v) + b_s.T[None, None, :, :, None]
    return u * s.reshape(Bn, T, B_WIDTH)


def setup_inputs(seed: int = 0) -> dict:
    key = jax.random.key(seed)
    ks = jax.random.split(key, 20)
    nrm = jax.random.normal
    f32 = jnp.float32
    D, L = D_MODEL, DEPTH
    return {
        'x': nrm(ks[0], (BATCH, SEQ, D), f32),
        'c': nrm(ks[1], (BATCH, D), f32),
        'w_ada': nrm(ks[2], (L, D, N_MOD * D), f32) * (0.5 * D ** -0.5),
        'b_ada': nrm(ks[3], (L, N_MOD * D), f32) * 0.02,
        'norm1_g': 1.0 + 0.02 * nrm(ks[4], (L, D), f32),
        'norm2_g': 1.0 + 0.02 * nrm(ks[5], (L, D), f32),
        'w_in': nrm(ks[6], (L, D, IN_WIDTH), f32) * D ** -0.5,
        'hgrn_lower_bounds': 0.5 * nrm(ks[7], (L, A_KEY_WIDTH), f32),
        'hgrn_norm_g': 1.0 + 0.02 * nrm(ks[8], (L, A_VAL_DIM), f32),
        'ln_v_g': 1.0 + 0.02 * nrm(ks[9], (L, B_WIDTH), f32),
        'ln_v_b': 0.02 * nrm(ks[10], (L, B_WIDTH), f32),
        'w_spatial': nrm(ks[11], (L, B_GROUPS, B_CHUNK, B_CHUNK), f32) * B_CHUNK ** -0.5,
        'b_spatial': 1.0 + 0.02 * nrm(ks[12], (L, B_GROUPS, B_CHUNK), f32),
        'w_proj_a': nrm(ks[13], (L, A_VAL_WIDTH, D), f32) * A_VAL_WIDTH ** -0.5,
        'w_proj_b': nrm(ks[14], (L, B_WIDTH, D), f32) * B_WIDTH ** -0.5,
        'w_out': nrm(ks[15], (L, D, D), f32) * D ** -0.5,
        'w_ff_in': nrm(ks[16], (L, D, FF_HIDDEN), f32) * D ** -0.5,
        'w_ff_out': nrm(ks[17], (L, FF_HIDDEN, D), f32) * FF_HIDDEN ** -0.5,
        'final_norm_g': 1.0 + 0.02 * nrm(ks[18], (D,), f32),
    }


def reference(x, c, w_ada, b_ada, norm1_g, norm2_g, w_in, hgrn_lower_bounds, hgrn_norm_g,
              ln_v_g, ln_v_b, w_spatial, b_spatial, w_proj_a, w_proj_b, w_out,
              w_ff_in, w_ff_out, final_norm_g):
    split_pts = _split_points()
    p = jax.nn.softmax(hgrn_lower_bounds.astype(jnp.float32), axis=0)
    lower_bounds = jnp.cumsum(p, axis=0) - p[0:1]
    c_act = jax.nn.silu(c)
    for l in range(DEPTH):
        mod = c_act @ w_ada[l] + b_ada[l]
        sh1, sc1, gt1, sh2, sc2, gt2 = jnp.split(mod, N_MOD, axis=-1)
        h = modulate(rms_norm(x, norm1_g[l]), sh1, sc1)
        proj = h @ w_in[l]
        q_r, f_r, i_r, g_r, u_r, v_r, ga_r, gb_r = jnp.split(proj, split_pts, axis=-1)
        y_a = hgrn2_mixer(q_r, f_r, i_r, g_r, lower_bounds[l], hgrn_norm_g[l]) @ w_proj_a[l]
        y_b = spatial_gating(u_r, v_r, ln_v_g[l], ln_v_b[l], w_spatial[l], b_spatial[l]) @ w_proj_b[l]
        merged = jax.nn.sigmoid(ga_r) * y_a + jax.nn.sigmoid(gb_r) * y_b
        x = x + gt1[:, None, :] * (merged @ w_out[l])
        h = modulate(rms_norm(x, norm2_g[l]), sh2, sc2)
        ff = jnp.square(jax.nn.relu(h @ w_ff_in[l])) @ w_ff_out[l]
        x = x + gt2[:, None, :] * ff
    return rms_norm(x, final_norm_g)
```

```python
import functools

import jax
import jax.numpy as jnp
from jax import lax
from jax.experimental import pallas as pl
from jax.experimental.pallas import tpu as pltpu

F32 = jnp.float32
BF16 = jnp.bfloat16

EPS = 1e-6
MIN_FORGET = 1e-20
N_MOD = 6
HEAD_DIM = 128
GROUP_DIM = 128
SGU_CHUNK = 128
HG_CHUNK = 128
HG_MID = 32
HG_SUB = 8
VMEM_LIMIT = 56 * 1024 * 1024


def _cparams(sem):
    return pltpu.CompilerParams(dimension_semantics=sem, vmem_limit_bytes=VMEM_LIMIT)


def _silu(x):
    return x * jax.nn.sigmoid(x)


def _gelu(x):
    return 0.5 * x * (1.0 + lax.erf(x * (0.5 ** 0.5)))


def _modnorm(x, g, shift, scale):
    ms = jnp.mean(x * x, axis=-1, keepdims=True)
    y = x * lax.rsqrt(ms + EPS) * g
    return y * (1.0 + scale) + shift


def _mod_kernel(c_ref, w_ref, b_ref, o_ref):
    c = c_ref[...]
    ca = _silu(c).astype(BF16)
    o_ref[...] = jnp.dot(ca, w_ref[...].astype(BF16), preferred_element_type=F32) + b_ref[...]


def _modulation(c, w_ada, b_ada):
    L, D, W = w_ada.shape
    B = c.shape[0]
    tn = min(1024, W)
    return pl.pallas_call(
        _mod_kernel,
        grid=(L, W // tn),
        in_specs=[
            pl.BlockSpec((B, D), lambda l, j: (0, 0)),
            pl.BlockSpec((None, D, tn), lambda l, j: (l, 0, j)),
            pl.BlockSpec((None, 1, tn), lambda l, j: (l, 0, j)),
        ],
        out_specs=pl.BlockSpec((None, B, tn), lambda l, j: (l, 0, j)),
        out_shape=jax.ShapeDtypeStruct((L, B, W), F32),
        compiler_params=_cparams(("arbitrary", "arbitrary")),
        name="adaln_modulation",
    )(c, w_ada, b_ada.reshape(L, 1, W))


def _lower_bounds_kernel(lb_ref, o_ref):
    x = lb_ref[...]
    m = jnp.max(x, axis=0, keepdims=True)
    e = jnp.exp(x - m)
    p = e / jnp.sum(e, axis=0, keepdims=True)
    L = x.shape[0]
    acc = jnp.zeros_like(p[0:1])
    rows = []
    for l in range(L):
        acc = acc + p[l:l + 1]
        rows.append(acc - p[0:1])
    o_ref[...] = jnp.concatenate(rows, axis=0)


def _lower_bounds(lb):
    return pl.pallas_call(
        _lower_bounds_kernel,
        out_shape=jax.ShapeDtypeStruct(lb.shape, F32),
        name="hgrn_lower_bounds",
    )(lb)


def _inproj_kernel(x_ref, sh_ref, sc_ref, g_ref, w_ref, o_ref, h_ref):
    @pl.when(pl.program_id(1) == 0)
    def _():
        h_ref[...] = _modnorm(x_ref[...], g_ref[...], sh_ref[...], sc_ref[...]).astype(BF16)

    o_ref[...] = jnp.dot(h_ref[...], w_ref[...], preferred_element_type=F32).astype(o_ref.dtype)


def _inproj(x, mod, norm_g, w, *, seq):
    N, D = x.shape
    W = w.shape[1]
    tm = min(1024, seq)
    tn = min(1024, W)
    per_b = seq // tm
    return pl.pallas_call(
        _inproj_kernel,
        grid=(N // tm, W // tn),
        in_specs=[
            pl.BlockSpec((tm, D), lambda i, j: (i, 0)),
            pl.BlockSpec((None, 1, D), lambda i, j: ((i // per_b) * N_MOD + 0, 0, 0)),
            pl.BlockSpec((None, 1, D), lambda i, j: ((i // per_b) * N_MOD + 1, 0, 0)),
            pl.BlockSpec((1, D), lambda i, j: (0, 0)),
            pl.BlockSpec((D, tn), lambda i, j: (0, j)),
        ],
        out_specs=pl.BlockSpec((tm, tn), lambda i, j: (i, j)),
        out_shape=jax.ShapeDtypeStruct((N, W), BF16),
        scratch_shapes=[pltpu.VMEM((tm, D), BF16)],
        compiler_params=_cparams(("parallel", "arbitrary")),
        name="in_projection",
    )(x, mod, mod, norm_g.reshape(1, D), w)


def _block_last(b, m):
    n, lanes = b.shape[0] // m, b.shape[1]
    b3 = b.reshape(n, m, lanes)
    return jnp.broadcast_to(b3[:, m - 1:m, :], (n, m, lanes)).reshape(b.shape)


def _dot_nt(a, b):
    return lax.dot_general(a, b, (((1,), (1,)), ((), ())), preferred_element_type=F32)


def _dot_tn(a, b):
    return lax.dot_general(a, b, (((0,), (0,)), ((), ())), preferred_element_type=F32)


def _hgrn_kernel(q_ref, f_ref, i_ref, g_ref, lb_ref, ng_ref, o_ref, st_ref, b_ref, k_ref, fc_ref):
    C = HG_CHUNK
    n_heads = st_ref.shape[0]

    @pl.when(pl.program_id(1) == 0)
    def _():
        st_ref[...] = jnp.zeros_like(st_ref)

    fr = f_ref[...].astype(F32)
    lb = lb_ref[...]
    sig = jax.nn.sigmoid(fr)
    fc = jnp.maximum(lb + (1.0 - lb) * sig, MIN_FORGET)
    lf = jnp.log(fc)
    k_ref[...] = (1.0 - lb) * (1.0 - sig)
    fc_ref[...] = fc
    r_i = lax.broadcasted_iota(jnp.int32, (C, C), 0)
    c_i = lax.broadcasted_iota(jnp.int32, (C, C), 1)
    tri = jnp.where(r_i >= c_i, 1.0, 0.0).astype(BF16)
    lf_hi = lf.astype(BF16)
    lf_lo = (lf - lf_hi.astype(F32)).astype(BF16)
    b_ref[...] = (jnp.dot(tri, lf_hi, preferred_element_type=F32)
                  + jnp.dot(tri, lf_lo, preferred_element_type=F32))

    row = lax.broadcasted_iota(jnp.int32, (C, 1), 0)
    row_mid = row // HG_MID
    row_in_mid = row % HG_MID
    same_mid = (r_i // HG_MID) == (c_i // HG_MID)
    n_mid = C // HG_MID
    n_sub = HG_MID // HG_SUB
    sub3 = lax.broadcasted_iota(jnp.int32, (C // HG_SUB, HG_SUB, 1), 1)
    ng = ng_ref[...]

    def head(h, carry):
        sl = pl.ds(pl.multiple_of(h * HEAD_DIM, HEAD_DIM), HEAD_DIM)
        b = b_ref[:, sl]
        k = k_ref[:, sl]
        fcs = fc_ref[:, sl]
        q = _silu(q_ref[:, sl].astype(F32))
        v = i_ref[:, sl].astype(F32)
        vb = v.astype(BF16)
        b_last = b[C - 1:C, :]

        q0 = (q * jnp.exp(b)).astype(BF16)
        k_mid = k * jnp.exp(_block_last(b, HG_MID) - b)
        qs, ks = [], []
        for j in range(1, n_mid):
            r = b[HG_MID * j - 1:HG_MID * j, :]
            qs.append(jnp.where(row >= HG_MID * j, q * jnp.exp(jnp.minimum(b - r, 0.0)), 0.0))
            ks.append(jnp.where(row_mid == j - 1, k_mid, 0.0))
        s1 = _dot_nt(jnp.concatenate(qs, axis=1).astype(BF16), jnp.concatenate(ks, axis=1).astype(BF16))
        k_sub = k * jnp.exp(_block_last(b, HG_SUB) - b)
        b4 = b.reshape(n_mid, HG_MID, HEAD_DIM)
        qs, ks = [], []
        for j in range(1, n_sub):
            r = jnp.broadcast_to(b4[:, HG_SUB * j - 1:HG_SUB * j, :], b4.shape).reshape(b.shape)
            qs.append(jnp.where(row_in_mid >= HG_SUB * j, q * jnp.exp(jnp.minimum(b - r, 0.0)), 0.0))
            ks.append(jnp.where(row_in_mid // HG_SUB == j - 1, k_sub, 0.0))
        s2 = _dot_nt(jnp.concatenate(qs, axis=1).astype(BF16), jnp.concatenate(ks, axis=1).astype(BF16))
        scores = (s1 + jnp.where(same_mid, s2, 0.0)).astype(BF16)

        st = st_ref[h]
        o = _dot_nt(q0, st.astype(BF16)) + jnp.dot(scores, vb, preferred_element_type=F32)

        shp3 = (C // HG_SUB, HG_SUB, HEAD_DIM)
        q3, k3, f3, v3 = q.reshape(shp3), k.reshape(shp3), fcs.reshape(shp3), v.reshape(shp3)
        band = jnp.sum(q3 * k3, axis=-1, keepdims=True) * v3
        dec = jnp.ones(shp3, F32)
        for d in range(1, HG_SUB):
            dec = dec * (f3 if d == 1 else pltpu.roll(f3, d - 1, 1))
            sd = jnp.sum(q3 * pltpu.roll(k3, d, 1) * dec, axis=-1, keepdims=True)
            band = band + jnp.where(sub3 >= d, sd, 0.0) * pltpu.roll(v3, d, 1)
        o = o + band.reshape(C, HEAD_DIM)

        k_end = (k * jnp.exp(b_last - b)).astype(BF16)
        st_ref[h] = st * jnp.exp(b_last) + _dot_tn(vb, k_end)

        ms = jnp.mean(o * o, axis=-1, keepdims=True)
        y = o * lax.rsqrt(ms + EPS) * ng
        y = y * _silu(g_ref[:, sl].astype(F32))
        o_ref[:, sl] = y.astype(o_ref.dtype)
        return carry

    lax.fori_loop(0, n_heads, head, 0)


def _hgrn(p, lb, norm_g, *, batch, seq, key_width, val_width):
    assert key_width == val_width
    N = p.shape[0]
    C = HG_CHUNK
    n_heads = key_width // HEAD_DIM
    per_b = seq // C
    col = lambda g: (lambda bi, t: (bi * per_b + t, g))
    return pl.pallas_call(
        _hgrn_kernel,
        grid=(batch, per_b),
        in_specs=[
            pl.BlockSpec((C, key_width), col(0)),
            pl.BlockSpec((C, key_width), col(1)),
            pl.BlockSpec((C, key_width), col(2)),
            pl.BlockSpec((C, key_width), col(3)),
            pl.BlockSpec((1, key_width), lambda bi, t: (0, 0)),
            pl.BlockSpec((1, HEAD_DIM), lambda bi, t: (0, 0)),
        ],
        out_specs=pl.BlockSpec((C, val_width), lambda bi, t: (bi * per_b + t, 0)),
        out_shape=jax.ShapeDtypeStruct((N, val_width), BF16),
        scratch_shapes=[
            pltpu.VMEM((n_heads, HEAD_DIM, HEAD_DIM), F32),
            pltpu.VMEM((C, key_width), F32),
            pltpu.VMEM((C, key_width), F32),
            pltpu.VMEM((C, key_width), F32),
        ],
        compiler_params=_cparams(("parallel", "arbitrary")),
        name="hgrn2_mixer",
    )(p, p, p, p, lb.reshape(1, key_width), norm_g.reshape(1, HEAD_DIM))


def _sgu_kernel(u_ref, v_ref, lng_ref, lnb_ref, w_ref, bs_ref, o_ref):
    C = SGU_CHUNK
    n_groups = w_ref.shape[0]
    r_i = lax.broadcasted_iota(jnp.int32, (C, C), 0)
    c_i = lax.broadcasted_iota(jnp.int32, (C, C), 1)
    tril = jnp.where(r_i >= c_i, 1.0, 0.0)
    for ci in range(u_ref.shape[0] // C):
        rows = pl.ds(ci * C, C)
        vg = _gelu(v_ref[rows, :].astype(F32))
        mu = jnp.mean(vg, axis=-1, keepdims=True)
        vc = vg - mu
        var = jnp.mean(vc * vc, axis=-1, keepdims=True)
        vn = (vc * lax.rsqrt(var + EPS) * lng_ref[...] + lnb_ref[...]).astype(BF16)
        for g in range(n_groups):
            cols = pl.ds(g * GROUP_DIM, GROUP_DIM)
            w = (w_ref[g] * tril).astype(BF16)
            s = jnp.dot(w, vn[:, g * GROUP_DIM:(g + 1) * GROUP_DIM], preferred_element_type=F32)
            s = s + bs_ref[:, g:g + 1]
            u = _gelu(u_ref[rows, cols].astype(F32))
            o_ref[rows, cols] = (u * s).astype(o_ref.dtype)


def _sgu(p, ln_g, ln_b, w_s, b_s, *, seq, col_u, width):
    N = p.shape[0]
    G, C, _ = w_s.shape
    assert C == SGU_CHUNK and G * GROUP_DIM == width and col_u % width == 0
    tt = min(512, seq)
    cu = col_u // width
    return pl.pallas_call(
        _sgu_kernel,
        grid=(N // tt,),
        in_specs=[
            pl.BlockSpec((tt, width), lambda i: (i, cu)),
            pl.BlockSpec((tt, width), lambda i: (i, cu + 1)),
            pl.BlockSpec((1, width), lambda i: (0, 0)),
            pl.BlockSpec((1, width), lambda i: (0, 0)),
            pl.BlockSpec((G, C, C), lambda i: (0, 0, 0)),
            pl.BlockSpec((C, G), lambda i: (0, 0)),
        ],
        out_specs=pl.BlockSpec((tt, width), lambda i: (i, 0)),
        out_shape=jax.ShapeDtypeStruct((N, width), BF16),
        compiler_params=_cparams(("parallel",)),
        name="gmlp_spatial_gating",
    )(p, p, ln_g.reshape(1, width), ln_b.reshape(1, width), w_s, b_s.T)


def _merge_kernel(oa_ref, ob_ref, ga_ref, gb_ref, wa_ref, wb_ref, wo_ref, x_ref, gt_ref,
                  g2_ref, sh2_ref, sc2_ref, x1_ref, h2_ref, acc_ref):
    kk = pl.program_id(1)
    ya = jnp.dot(oa_ref[...], wa_ref[...], preferred_element_type=F32)
    yb = jnp.dot(ob_ref[...], wb_ref[...], preferred_element_type=F32)
    m = jax.nn.sigmoid(ga_ref[...].astype(F32)) * ya + jax.nn.sigmoid(gb_ref[...].astype(F32)) * yb
    contrib = jnp.dot(m.astype(BF16), wo_ref[...], preferred_element_type=F32)

    @pl.when(kk == 0)
    def _():
        acc_ref[...] = contrib

    @pl.when(kk > 0)
    def _():
        acc_ref[...] += contrib

    @pl.when(kk == pl.num_programs(1) - 1)
    def _():
        x1 = x_ref[...] + gt_ref[...] * acc_ref[...]
        x1_ref[...] = x1
        h2_ref[...] = _modnorm(x1, g2_ref[...], sh2_ref[...], sc2_ref[...]).astype(BF16)


def _merge(x, p, oa, ob, wa, wb, wo, mod, norm2_g, *, seq, col_ga):
    N, D = x.shape
    Wa = oa.shape[1]
    tm = min(512, seq)
    tk = min(512, D)
    assert col_ga % tk == 0 and D % tk == 0
    per_b = seq // tm
    cga = col_ga // tk
    cgb = (col_ga + D) // tk
    modspec = lambda which: pl.BlockSpec((None, 1, D), lambda i, k: ((i // per_b) * N_MOD + which, 0, 0))
    return pl.pallas_call(
        _merge_kernel,
        grid=(N // tm, D // tk),
        in_specs=[
            pl.BlockSpec((tm, Wa), lambda i, k: (i, 0)),
            pl.BlockSpec((tm, Wa), lambda i, k: (i, 0)),
            pl.BlockSpec((tm, tk), lambda i, k: (i, cga + k)),
            pl.BlockSpec((tm, tk), lambda i, k: (i, cgb + k)),
            pl.BlockSpec((Wa, tk), lambda i, k: (0, k)),
            pl.BlockSpec((Wa, tk), lambda i, k: (0, k)),
            pl.BlockSpec((tk, D), lambda i, k: (k, 0)),
            pl.BlockSpec((tm, D), lambda i, k: (i, 0)),
            modspec(2),
            pl.BlockSpec((1, D), lambda i, k: (0, 0)),
            modspec(3),
            modspec(4),
        ],
        out_specs=[
            pl.BlockSpec((tm, D), lambda i, k: (i, 0)),
            pl.BlockSpec((tm, D), lambda i, k: (i, 0)),
        ],
        out_shape=[jax.ShapeDtypeStruct((N, D), F32), jax.ShapeDtypeStruct((N, D), BF16)],
        scratch_shapes=[pltpu.VMEM((tm, D), F32)],
        compiler_params=_cparams(("parallel", "arbitrary")),
        name="merge_out_projection",
    )(oa, ob, p, p, wa, wb, wo, x, mod, norm2_g.reshape(1, D), mod, mod)


def _ffn_kernel(h_ref, w1_ref, w2_ref, x_ref, gt_ref, fg_ref, o_ref, acc_ref, *, final_norm):
    kk = pl.program_id(1)
    hid = jnp.dot(h_ref[...], w1_ref[...], preferred_element_type=F32)
    act = jnp.square(jnp.maximum(hid, 0.0)).astype(BF16)
    contrib = jnp.dot(act, w2_ref[...], preferred_element_type=F32)

    @pl.when(kk == 0)
    def _():
        acc_ref[...] = contrib

    @pl.when(kk > 0)
    def _():
        acc_ref[...] += contrib

    @pl.when(kk == pl.num_programs(1) - 1)
    def _():
        x2 = x_ref[...] + gt_ref[...] * acc_ref[...]
        if final_norm:
            ms = jnp.mean(x2 * x2, axis=-1, keepdims=True)
            x2 = x2 * lax.rsqrt(ms + EPS) * fg_ref[...]
        o_ref[...] = x2


def _ffn(x1, h2, w1, w2, mod, final_g, *, seq, final_norm):
    N, D = x1.shape
    FF = w1.shape[1]
    tm = min(512, seq)
    th = min(512, FF)
    per_b = seq // tm
    return pl.pallas_call(
        functools.partial(_ffn_kernel, final_norm=final_norm),
        grid=(N // tm, FF // th),
        in_specs=[
            pl.BlockSpec((tm, D), lambda i, k: (i, 0)),
            pl.BlockSpec((D, th), lambda i, k: (0, k)),
            pl.BlockSpec((th, D), lambda i, k: (k, 0)),
            pl.BlockSpec((tm, D), lambda i, k: (i, 0)),
            pl.BlockSpec((None, 1, D), lambda i, k: ((i // per_b) * N_MOD + 5, 0, 0)),
            pl.BlockSpec((1, D), lambda i, k: (0, 0)),
        ],
        out_specs=pl.BlockSpec((tm, D), lambda i, k: (i, 0)),
        out_shape=jax.ShapeDtypeStruct((N, D), F32),
        scratch_shapes=[pltpu.VMEM((tm, D), F32)],
        compiler_params=_cparams(("parallel", "arbitrary")),
        name="relu2_mlp",
    )(h2, w1, w2, x1, mod, final_g.reshape(1, D))


def kernel(x, c, w_ada, b_ada, norm1_g, norm2_g, w_in, hgrn_lower_bounds, hgrn_norm_g, ln_v_g, ln_v_b,
           w_spatial, b_spatial, w_proj_a, w_proj_b, w_out, w_ff_in, w_ff_out, final_norm_g):
    B, T, D = x.shape
    L = w_ada.shape[0]
    N = B * T
    key_width = hgrn_lower_bounds.shape[1]
    val_width = w_proj_a.shape[1]
    sgu_width = w_proj_b.shape[1]
    col_u = 2 * key_width + 2 * val_width
    col_ga = col_u + 2 * sgu_width

    mod_all = _modulation(c, w_ada, b_ada)
    lbs = _lower_bounds(hgrn_lower_bounds)
    w_in_b, wa_b, wb_b = w_in.astype(BF16), w_proj_a.astype(BF16), w_proj_b.astype(BF16)
    wo_b, w1_b, w2_b = w_out.astype(BF16), w_ff_in.astype(BF16), w_ff_out.astype(BF16)

    xf = x.reshape(N, D)
    for l in range(L):
        mod = mod_all[l].reshape(B * N_MOD, 1, D)
        p = _inproj(xf, mod, norm1_g[l], w_in_b[l], seq=T)
        oa = _hgrn(p, lbs[l], hgrn_norm_g[l], batch=B, seq=T, key_width=key_width, val_width=val_width)
        ob = _sgu(p, ln_v_g[l], ln_v_b[l], w_spatial[l], b_spatial[l], seq=T, col_u=col_u, width=sgu_width)
        x1, h2 = _merge(xf, p, oa, ob, wa_b[l], wb_b[l], wo_b[l], mod, norm2_g[l], seq=T, col_ga=col_ga)
        xf = _ffn(x1, h2, w1_b[l], w2_b[l], mod, final_norm_g, seq=T, final_norm=(l == L - 1))
    return xf.reshape(B, T, D)
```

```python
import functools

import jax
import jax.numpy as jnp
from jax import lax
from jax.experimental import pallas as pl
from jax.experimental.pallas import tpu as pltpu

F32 = jnp.float32
BF16 = jnp.bfloat16

EPS = 1e-6
MIN_FORGET = 1e-20
N_MOD = 6
HEAD_DIM = 128
GROUP_DIM = 128
SGU_CHUNK = 128
HG_CHUNK = 128
HG_MID = 32
HG_SUB = 8
VMEM_LIMIT = 56 * 1024 * 1024


def _cparams(sem):
    return pltpu.CompilerParams(dimension_semantics=sem, vmem_limit_bytes=VMEM_LIMIT)


def _silu(x):
    return x * jax.nn.sigmoid(x)


def _gelu(x):
    return 0.5 * x * (1.0 + lax.erf(x * (0.5 ** 0.5)))


def _modnorm(x, g, shift, scale):
    ms = jnp.mean(x * x, axis=-1, keepdims=True)
    y = x * lax.rsqrt(ms + EPS) * g
    return y * (1.0 + scale) + shift


def _mod_kernel(c_ref, w_ref, b_ref, o_ref):
    c = c_ref[...]
    ca = _silu(c).astype(BF16)
    o_ref[...] = jnp.dot(ca, w_ref[...].astype(BF16), preferred_element_type=F32) + b_ref[...]


def _modulation(c, w_ada, b_ada):
    L, D, W = w_ada.shape
    B = c.shape[0]
    tn = min(1024, W)
    return pl.pallas_call(
        _mod_kernel,
        grid=(L, W // tn),
        in_specs=[
            pl.BlockSpec((B, D), lambda l, j: (0, 0)),
            pl.BlockSpec((None, D, tn), lambda l, j: (l, 0, j)),
            pl.BlockSpec((None, 1, tn), lambda l, j: (l, 0, j)),
        ],
        out_specs=pl.BlockSpec((None, B, tn), lambda l, j: (l, 0, j)),
        out_shape=jax.ShapeDtypeStruct((L, B, W), F32),
        compiler_params=_cparams(("arbitrary", "arbitrary")),
        name="adaln_modulation",
    )(c, w_ada, b_ada.reshape(L, 1, W))


def _lower_bounds_kernel(lb_ref, o_ref):
    x = lb_ref[...]
    m = jnp.max(x, axis=0, keepdims=True)
    e = jnp.exp(x - m)
    p = e / jnp.sum(e, axis=0, keepdims=True)
    L = x.shape[0]
    acc = jnp.zeros_like(p[0:1])
    rows = []
    for l in range(L):
        acc = acc + p[l:l + 1]
        rows.append(acc - p[0:1])
    o_ref[...] = jnp.concatenate(rows, axis=0)


def _lower_bounds(lb):
    return pl.pallas_call(
        _lower_bounds_kernel,
        out_shape=jax.ShapeDtypeStruct(lb.shape, F32),
        name="hgrn_lower_bounds",
    )(lb)


def _inproj_kernel(x_ref, sh_ref, sc_ref, g_ref, w_ref, o_ref, h_ref):
    @pl.when(pl.program_id(1) == 0)
    def _():
        h_ref[...] = _modnorm(x_ref[...], g_ref[...], sh_ref[...], sc_ref[...]).astype(BF16)

    o_ref[...] = jnp.dot(h_ref[...], w_ref[...], preferred_element_type=F32).astype(o_ref.dtype)


def _inproj(x, mod, norm_g, w, *, seq):
    N, D = x.shape
    W = w.shape[1]
    tm = min(1024, seq)
    tn = min(2048, W)
    per_b = seq // tm
    return pl.pallas_call(
        _inproj_kernel,
        grid=(N // tm, W // tn),
        in_specs=[
            pl.BlockSpec((tm, D), lambda i, j: (i, 0)),
            pl.BlockSpec((None, 1, D), lambda i, j: ((i // per_b) * N_MOD + 0, 0, 0)),
            pl.BlockSpec((None, 1, D), lambda i, j: ((i // per_b) * N_MOD + 1, 0, 0)),
            pl.BlockSpec((1, D), lambda i, j: (0, 0)),
            pl.BlockSpec((D, tn), lambda i, j: (0, j)),
        ],
        out_specs=pl.BlockSpec((tm, tn), lambda i, j: (i, j)),
        out_shape=jax.ShapeDtypeStruct((N, W), BF16),
        scratch_shapes=[pltpu.VMEM((tm, D), BF16)],
        compiler_params=_cparams(("parallel", "arbitrary")),
        name="in_projection",
    )(x, mod, mod, norm_g.reshape(1, D), w)


def _block_last(b, m):
    n, lanes = b.shape[0] // m, b.shape[1]
    b3 = b.reshape(n, m, lanes)
    return jnp.broadcast_to(b3[:, m - 1:m, :], (n, m, lanes)).reshape(b.shape)


def _dot_nt(a, b):
    return lax.dot_general(a, b, (((1,), (1,)), ((), ())), preferred_element_type=F32)


def _dot_tn(a, b):
    return lax.dot_general(a, b, (((0,), (0,)), ((), ())), preferred_element_type=F32)


def _hgrn_kernel(q_ref, f_ref, i_ref, g_ref, lb_ref, ng_ref, o_ref, st_ref, b_ref, k_ref, fc_ref):
    C = HG_CHUNK
    n_heads = st_ref.shape[0]

    @pl.when(pl.program_id(1) == 0)
    def _():
        st_ref[...] = jnp.zeros_like(st_ref)

    fr = f_ref[...].astype(F32)
    lb = lb_ref[...]
    sig = jax.nn.sigmoid(fr)
    fc = jnp.maximum(lb + (1.0 - lb) * sig, MIN_FORGET)
    lf = jnp.log(fc)
    k_ref[...] = (1.0 - lb) * (1.0 - sig)
    fc_ref[...] = fc
    r_i = lax.broadcasted_iota(jnp.int32, (C, C), 0)
    c_i = lax.broadcasted_iota(jnp.int32, (C, C), 1)
    tri = jnp.where(r_i >= c_i, 1.0, 0.0).astype(BF16)
    lf_hi = lf.astype(BF16)
    lf_lo = (lf - lf_hi.astype(F32)).astype(BF16)
    b_ref[...] = (jnp.dot(tri, lf_hi, preferred_element_type=F32)
                  + jnp.dot(tri, lf_lo, preferred_element_type=F32))

    row = lax.broadcasted_iota(jnp.int32, (C, 1), 0)
    row_mid = row // HG_MID
    row_in_mid = row % HG_MID
    same_mid = (r_i // HG_MID) == (c_i // HG_MID)
    n_mid = C // HG_MID
    n_sub = HG_MID // HG_SUB
    sub3 = lax.broadcasted_iota(jnp.int32, (C // HG_SUB, HG_SUB, 1), 1)
    ng = ng_ref[...]

    def head(h, carry):
        sl = pl.ds(pl.multiple_of(h * HEAD_DIM, HEAD_DIM), HEAD_DIM)
        b = b_ref[:, sl]
        k = k_ref[:, sl]
        fcs = fc_ref[:, sl]
        q = _silu(q_ref[:, sl].astype(F32))
        v = i_ref[:, sl].astype(F32)
        vb = v.astype(BF16)
        b_last = b[C - 1:C, :]

        q0 = (q * jnp.exp(b)).astype(BF16)
        k_mid = k * jnp.exp(_block_last(b, HG_MID) - b)
        qs, ks = [], []
        for j in range(1, n_mid):
            r = b[HG_MID * j - 1:HG_MID * j, :]
            qs.append(jnp.where(row >= HG_MID * j, q * jnp.exp(jnp.minimum(b - r, 0.0)), 0.0))
            ks.append(jnp.where(row_mid == j - 1, k_mid, 0.0))
        s1 = _dot_nt(jnp.concatenate(qs, axis=1).astype(BF16), jnp.concatenate(ks, axis=1).astype(BF16))
        k_sub = k * jnp.exp(_block_last(b, HG_SUB) - b)
        b4 = b.reshape(n_mid, HG_MID, HEAD_DIM)
        qs, ks = [], []
        for j in range(1, n_sub):
            r = jnp.broadcast_to(b4[:, HG_SUB * j - 1:HG_SUB * j, :], b4.shape).reshape(b.shape)
            qs.append(jnp.where(row_in_mid >= HG_SUB * j, q * jnp.exp(jnp.minimum(b - r, 0.0)), 0.0))
            ks.append(jnp.where(row_in_mid // HG_SUB == j - 1, k_sub, 0.0))
        s2 = _dot_nt(jnp.concatenate(qs, axis=1).astype(BF16), jnp.concatenate(ks, axis=1).astype(BF16))
        scores = (s1 + jnp.where(same_mid, s2, 0.0)).astype(BF16)

        st = st_ref[h]
        o = _dot_nt(q0, st.astype(BF16)) + jnp.dot(scores, vb, preferred_element_type=F32)

        shp3 = (C // HG_SUB, HG_SUB, HEAD_DIM)
        q3, k3, f3, v3 = q.reshape(shp3), k.reshape(shp3), fcs.reshape(shp3), v.reshape(shp3)
        band = jnp.sum(q3 * k3, axis=-1, keepdims=True) * v3
        dec = jnp.ones(shp3, F32)
        for d in range(1, HG_SUB):
            dec = dec * (f3 if d == 1 else pltpu.roll(f3, d - 1, 1))
            sd = jnp.sum(q3 * pltpu.roll(k3, d, 1) * dec, axis=-1, keepdims=True)
            band = band + jnp.where(sub3 >= d, sd, 0.0) * pltpu.roll(v3, d, 1)
        o = o + band.reshape(C, HEAD_DIM)

        k_end = (k * jnp.exp(b_last - b)).astype(BF16)
        st_ref[h] = st * jnp.exp(b_last) + _dot_tn(vb, k_end)

        ms = jnp.mean(o * o, axis=-1, keepdims=True)
        y = o * lax.rsqrt(ms + EPS) * ng
        y = y * _silu(g_ref[:, sl].astype(F32))
        o_ref[:, sl] = y.astype(o_ref.dtype)
        return carry

    lax.fori_loop(0, n_heads, head, 0)


def _hgrn(p, lb, norm_g, *, batch, seq, key_width, val_width):
    assert key_width == val_width
    N = p.shape[0]
    C = HG_CHUNK
    n_heads = key_width // HEAD_DIM
    per_b = seq // C
    col = lambda g: (lambda bi, t: (bi * per_b + t, g))
    return pl.pallas_call(
        _hgrn_kernel,
        grid=(batch, per_b),
        in_specs=[
            pl.BlockSpec((C, key_width), col(0)),
            pl.BlockSpec((C, key_width), col(1)),
            pl.BlockSpec((C, key_width), col(2)),
            pl.BlockSpec((C, key_width), col(3)),
            pl.BlockSpec((1, key_width), lambda bi, t: (0, 0)),
            pl.BlockSpec((1, HEAD_DIM), lambda bi, t: (0, 0)),
        ],
        out_specs=pl.BlockSpec((C, val_width), lambda bi, t: (bi * per_b + t, 0)),
        out_shape=jax.ShapeDtypeStruct((N, val_width), BF16),
        scratch_shapes=[
            pltpu.VMEM((n_heads, HEAD_DIM, HEAD_DIM), F32),
            pltpu.VMEM((C, key_width), F32),
            pltpu.VMEM((C, key_width), F32),
            pltpu.VMEM((C, key_width), F32),
        ],
        compiler_params=_cparams(("parallel", "arbitrary")),
        name="hgrn2_mixer",
    )(p, p, p, p, lb.reshape(1, key_width), norm_g.reshape(1, HEAD_DIM))


def _sgu_kernel(u_ref, v_ref, lng_ref, lnb_ref, w_ref, bs_ref, o_ref):
    C = SGU_CHUNK
    n_groups = w_ref.shape[0]
    r_i = lax.broadcasted_iota(jnp.int32, (C, C), 0)
    c_i = lax.broadcasted_iota(jnp.int32, (C, C), 1)
    tril = jnp.where(r_i >= c_i, 1.0, 0.0)
    for ci in range(u_ref.shape[0] // C):
        rows = pl.ds(ci * C, C)
        vg = _gelu(v_ref[rows, :].astype(F32))
        mu = jnp.mean(vg, axis=-1, keepdims=True)
        vc = vg - mu
        var = jnp.mean(vc * vc, axis=-1, keepdims=True)
        vn = (vc * lax.rsqrt(var + EPS) * lng_ref[...] + lnb_ref[...]).astype(BF16)
        for g in range(n_groups):
            cols = pl.ds(g * GROUP_DIM, GROUP_DIM)
            w = (w_ref[g] * tril).astype(BF16)
            s = jnp.dot(w, vn[:, g * GROUP_DIM:(g + 1) * GROUP_DIM], preferred_element_type=F32)
            s = s + bs_ref[:, g:g + 1]
            u = _gelu(u_ref[rows, cols].astype(F32))
            o_ref[rows, cols] = (u * s).astype(o_ref.dtype)


def _sgu(p, ln_g, ln_b, w_s, b_s, *, seq, col_u, width):
    N = p.shape[0]
    G, C, _ = w_s.shape
    assert C == SGU_CHUNK and G * GROUP_DIM == width and col_u % width == 0
    tt = min(512, seq)
    cu = col_u // width
    return pl.pallas_call(
        _sgu_kernel,
        grid=(N // tt,),
        in_specs=[
            pl.BlockSpec((tt, width), lambda i: (i, cu)),
            pl.BlockSpec((tt, width), lambda i: (i, cu + 1)),
            pl.BlockSpec((1, width), lambda i: (0, 0)),
            pl.BlockSpec((1, width), lambda i: (0, 0)),
            pl.BlockSpec((G, C, C), lambda i: (0, 0, 0)),
            pl.BlockSpec((C, G), lambda i: (0, 0)),
        ],
        out_specs=pl.BlockSpec((tt, width), lambda i: (i, 0)),
        out_shape=jax.ShapeDtypeStruct((N, width), BF16),
        compiler_params=_cparams(("parallel",)),
        name="gmlp_spatial_gating",
    )(p, p, ln_g.reshape(1, width), ln_b.reshape(1, width), w_s, b_s.T)


def _merge_kernel(oa_ref, ob_ref, ga_ref, gb_ref, wa_ref, wb_ref, wo_ref, x_ref, gt_ref, x1_ref):
    @pl.when(pl.program_id(1) == 0)
    def _():
        x1_ref[...] = x_ref[...]

    ya = jnp.dot(oa_ref[...], wa_ref[...], preferred_element_type=F32)
    yb = jnp.dot(ob_ref[...], wb_ref[...], preferred_element_type=F32)
    m = jax.nn.sigmoid(ga_ref[...].astype(F32)) * ya + jax.nn.sigmoid(gb_ref[...].astype(F32)) * yb
    x1_ref[...] += gt_ref[...] * jnp.dot(m.astype(BF16), wo_ref[...], preferred_element_type=F32)


def _merge(x, p, oa, ob, wa, wb, wo, mod, *, seq, col_ga):
    N, D = x.shape
    Wa = oa.shape[1]
    tm = min(512, seq)
    tk = min(1024, D)
    assert col_ga % tk == 0 and D % tk == 0
    per_b = seq // tm
    cga = col_ga // tk
    cgb = (col_ga + D) // tk
    return pl.pallas_call(
        _merge_kernel,
        grid=(N // tm, D // tk),
        in_specs=[
            pl.BlockSpec((tm, Wa), lambda i, k: (i, 0)),
            pl.BlockSpec((tm, Wa), lambda i, k: (i, 0)),
            pl.BlockSpec((tm, tk), lambda i, k: (i, cga + k)),
            pl.BlockSpec((tm, tk), lambda i, k: (i, cgb + k)),
            pl.BlockSpec((Wa, tk), lambda i, k: (0, k)),
            pl.BlockSpec((Wa, tk), lambda i, k: (0, k)),
            pl.BlockSpec((tk, D), lambda i, k: (k, 0)),
            pl.BlockSpec((tm, D), lambda i, k: (i, 0)),
            pl.BlockSpec((None, 1, D), lambda i, k: ((i // per_b) * N_MOD + 2, 0, 0)),
        ],
        out_specs=pl.BlockSpec((tm, D), lambda i, k: (i, 0)),
        out_shape=jax.ShapeDtypeStruct((N, D), F32),
        compiler_params=_cparams(("parallel", "arbitrary")),
        name="merge_out_projection",
    )(oa, ob, p, p, wa, wb, wo, x, mod)


def _ffn_kernel(x_ref, g_ref, sh_ref, sc_ref, gt_ref, w1_ref, w2_ref, fg_ref, o_ref, h_ref, *, final_norm):
    kk = pl.program_id(1)

    @pl.when(kk == 0)
    def _():
        x = x_ref[...]
        h_ref[...] = _modnorm(x, g_ref[...], sh_ref[...], sc_ref[...]).astype(BF16)
        o_ref[...] = x

    hid = jnp.dot(h_ref[...], w1_ref[...], preferred_element_type=F32)
    act = jnp.square(jnp.maximum(hid, 0.0)).astype(BF16)
    o_ref[...] += gt_ref[...] * jnp.dot(act, w2_ref[...], preferred_element_type=F32)

    if final_norm:
        @pl.when(kk == pl.num_programs(1) - 1)
        def _():
            x2 = o_ref[...]
            ms = jnp.mean(x2 * x2, axis=-1, keepdims=True)
            o_ref[...] = x2 * lax.rsqrt(ms + EPS) * fg_ref[...]


def _ffn(x1, w1, w2, mod, norm2_g, final_g, *, seq, final_norm):
    N, D = x1.shape
    FF = w1.shape[1]
    tm = min(1024, seq)
    th = min(512, FF)
    per_b = seq // tm
    modspec = lambda which: pl.BlockSpec((None, 1, D), lambda i, k: ((i // per_b) * N_MOD + which, 0, 0))
    return pl.pallas_call(
        functools.partial(_ffn_kernel, final_norm=final_norm),
        grid=(N // tm, FF // th),
        in_specs=[
            pl.BlockSpec((tm, D), lambda i, k: (i, 0)),
            pl.BlockSpec((1, D), lambda i, k: (0, 0)),
            modspec(3),
            modspec(4),
            modspec(5),
            pl.BlockSpec((D, th), lambda i, k: (0, k)),
            pl.BlockSpec((th, D), lambda i, k: (k, 0)),
            pl.BlockSpec((1, D), lambda i, k: (0, 0)),
        ],
        out_specs=pl.BlockSpec((tm, D), lambda i, k: (i, 0)),
        out_shape=jax.ShapeDtypeStruct((N, D), F32),
        scratch_shapes=[pltpu.VMEM((tm, D), BF16)],
        compiler_params=_cparams(("parallel", "arbitrary")),
        name="relu2_mlp",
    )(x1, norm2_g.reshape(1, D), mod, mod, mod, w1, w2, final_g.reshape(1, D))


def kernel(x, c, w_ada, b_ada, norm1_g, norm2_g, w_in, hgrn_lower_bounds, hgrn_norm_g, ln_v_g, ln_v_b,
           w_spatial, b_spatial, w_proj_a, w_proj_b, w_out, w_ff_in, w_ff_out, final_norm_g):
    B, T, D = x.shape
    L = w_ada.shape[0]
    N = B * T
    key_width = hgrn_lower_bounds.shape[1]
    val_width = w_proj_a.shape[1]
    sgu_width = w_proj_b.shape[1]
    col_u = 2 * key_width + 2 * val_width
    col_ga = col_u + 2 * sgu_width

    mod_all = _modulation(c, w_ada, b_ada)
    lbs = _lower_bounds(hgrn_lower_bounds)
    w_in_b, wa_b, wb_b = w_in.astype(BF16), w_proj_a.astype(BF16), w_proj_b.astype(BF16)
    wo_b, w1_b, w2_b = w_out.astype(BF16), w_ff_in.astype(BF16), w_ff_out.astype(BF16)

    xf = x.reshape(N, D)
    for l in range(L):
        mod = mod_all[l].reshape(B * N_MOD, 1, D)
        p = _inproj(xf, mod, norm1_g[l], w_in_b[l], seq=T)
        oa = _hgrn(p, lbs[l], hgrn_norm_g[l], batch=B, seq=T, key_width=key_width, val_width=val_width)
        ob = _sgu(p, ln_v_g[l], ln_v_b[l], w_spatial[l], b_spatial[l], seq=T, col_u=col_u, width=sgu_width)
        x1 = _merge(xf, p, oa, ob, wa_b[l], wb_b[l], wo_b[l], mod, seq=T, col_ga=col_ga)
        xf = _ffn(x1, w1_b[l], w2_b[l], mod, norm2_g[l], final_norm_g, seq=T, final_norm=(l == L - 1))
    return xf.reshape(B, T, D)
```

```python
import functools

import jax
import jax.numpy as jnp
from jax import lax
from jax.experimental import pallas as pl
from jax.experimental.pallas import tpu as pltpu

F32 = jnp.float32
BF16 = jnp.bfloat16

EPS = 1e-6
MIN_FORGET = 1e-20
N_MOD = 6
HEAD_DIM = 128
GROUP_DIM = 128
SGU_CHUNK = 128
HG_CHUNK = 128
HG_MID = 32
HG_SUB = 8
HG_ROWS = 512
VMEM_LIMIT = 56 * 1024 * 1024
INPROJ_TN = 2048
MERGE_TK = 1024
FFN_TH = 512


def _chunk_cols(w, t):
    L, R, C = w.shape
    t = min(t, C)
    return w.reshape(L, R, C // t, t).transpose(0, 2, 1, 3).astype(BF16)


def _cparams(sem):
    return pltpu.CompilerParams(dimension_semantics=sem, vmem_limit_bytes=VMEM_LIMIT)


def _silu(x):
    return x * jax.nn.sigmoid(x)


def _gelu(x):
    return 0.5 * x * (1.0 + lax.erf(x * (0.5 ** 0.5)))


def _modnorm(x, g, shift, scale):
    ms = jnp.mean(x * x, axis=-1, keepdims=True)
    y = x * lax.rsqrt(ms + EPS) * g
    return y * (1.0 + scale) + shift


def _mod_kernel(c_ref, w_ref, b_ref, o_ref):
    c = c_ref[...]
    ca = _silu(c).astype(BF16)
    o_ref[...] = jnp.dot(ca, w_ref[...].astype(BF16), preferred_element_type=F32) + b_ref[...]


def _modulation(c, w_ada, b_ada):
    L, D, W = w_ada.shape
    B = c.shape[0]
    tn = min(1024, W)
    return pl.pallas_call(
        _mod_kernel,
        grid=(L, W // tn),
        in_specs=[
            pl.BlockSpec((B, D), lambda l, j: (0, 0)),
            pl.BlockSpec((None, D, tn), lambda l, j: (l, 0, j)),
            pl.BlockSpec((None, 1, tn), lambda l, j: (l, 0, j)),
        ],
        out_specs=pl.BlockSpec((None, B, tn), lambda l, j: (l, 0, j)),
        out_shape=jax.ShapeDtypeStruct((L, B, W), F32),
        compiler_params=_cparams(("arbitrary", "arbitrary")),
        name="adaln_modulation",
    )(c, w_ada, b_ada.reshape(L, 1, W))


def _lower_bounds_kernel(lb_ref, o_ref):
    x = lb_ref[...]
    m = jnp.max(x, axis=0, keepdims=True)
    e = jnp.exp(x - m)
    p = e / jnp.sum(e, axis=0, keepdims=True)
    L = x.shape[0]
    acc = jnp.zeros_like(p[0:1])
    rows = []
    for l in range(L):
        acc = acc + p[l:l + 1]
        rows.append(acc - p[0:1])
    o_ref[...] = jnp.concatenate(rows, axis=0)


def _lower_bounds(lb):
    return pl.pallas_call(
        _lower_bounds_kernel,
        out_shape=jax.ShapeDtypeStruct(lb.shape, F32),
        name="hgrn_lower_bounds",
    )(lb)


def _inproj_kernel(x_ref, sh_ref, sc_ref, g_ref, w_ref, o_ref, h_ref):
    @pl.when(pl.program_id(1) == 0)
    def _():
        h_ref[...] = _modnorm(x_ref[...], g_ref[...], sh_ref[...], sc_ref[...]).astype(BF16)

    o_ref[...] = jnp.dot(h_ref[...], w_ref[...], preferred_element_type=F32).astype(o_ref.dtype)


def _inproj(x, mod, norm_g, w, l, *, seq):
    N, D = x.shape
    tn = w.shape[3]
    W = w.shape[1] * tn
    tm = min(1024, seq)
    per_b = seq // tm
    return pl.pallas_call(
        _inproj_kernel,
        grid=(N // tm, W // tn),
        in_specs=[
            pl.BlockSpec((tm, D), lambda i, j: (i, 0)),
            pl.BlockSpec((None, 1, D), lambda i, j: ((i // per_b) * N_MOD + 0, 0, 0)),
            pl.BlockSpec((None, 1, D), lambda i, j: ((i // per_b) * N_MOD + 1, 0, 0)),
            pl.BlockSpec((1, D), lambda i, j: (0, 0)),
            pl.BlockSpec((None, None, D, tn), lambda i, j: (l, j, 0, 0)),
        ],
        out_specs=pl.BlockSpec((tm, tn), lambda i, j: (i, j)),
        out_shape=jax.ShapeDtypeStruct((N, W), BF16),
        scratch_shapes=[pltpu.VMEM((tm, D), BF16)],
        compiler_params=_cparams(("parallel", "arbitrary")),
        name="in_projection",
    )(x, mod, mod, norm_g.reshape(1, D), w)


def _dot_nt(a, b):
    return lax.dot_general(a, b, (((1,), (1,)), ((), ())), preferred_element_type=F32)


def _dot_tn(a, b):
    return lax.dot_general(a, b, (((0,), (0,)), ((), ())), preferred_element_type=F32)


def _hgrn_head(qr, fr, vr, gr, lb, ng, st, sub, same_mid):
    C = HG_CHUNK
    nb = C // HG_SUB
    n_mid = C // HG_MID
    n_sub = HG_MID // HG_SUB
    shp3 = (nb, HG_SUB, HEAD_DIM)

    sig = jax.nn.sigmoid(fr)
    fc = jnp.maximum(lb + (1.0 - lb) * sig, MIN_FORGET)
    k3 = ((1.0 - lb) * (1.0 - sig)).reshape(shp3)
    f3 = fc.reshape(shp3)
    q3 = _silu(qr).reshape(shp3)
    v3 = vr.reshape(shp3)
    vb = vr.astype(BF16)

    a = jnp.log(f3)
    for sh in (1, 2, 4):
        a = a + jnp.where(sub >= sh, pltpu.roll(a, sh, 1), 0.0)
    tot = a[:, HG_SUB - 1:HG_SUB, :]
    e8 = jnp.exp(a)
    qe = q3 * e8
    ke = k3 * jnp.exp(tot - a)
    g = [e8[j, HG_SUB - 1:HG_SUB, :] for j in range(nb)]

    def bc(r):
        return jnp.broadcast_to(r, (HG_SUB, HEAD_DIM))

    pre, suf, g_mid = [], [], []
    for mi in range(n_mid):
        gs = g[mi * n_sub:(mi + 1) * n_sub]
        p = [None]
        for i in range(1, n_sub):
            p.append(gs[i - 1] if p[-1] is None else p[-1] * gs[i - 1])
        s = [None] * n_sub
        for i in range(n_sub - 2, -1, -1):
            s[i] = gs[i + 1] if s[i + 1] is None else s[i + 1] * gs[i + 1]
        pre.append(p)
        suf.append(s)
        g_mid.append(p[-1] * gs[-1])

    zero = jnp.zeros((HG_SUB, HEAD_DIM), F32)
    q2 = [[zero] * nb for _ in range(n_sub - 1)]
    k2 = [[zero] * nb for _ in range(n_sub - 1)]
    q1 = [[zero] * nb for _ in range(n_mid - 1)]
    k1 = [[zero] * nb for _ in range(n_mid - 1)]
    q0, kend = [None] * nb, [None] * nb
    for mi in range(n_mid):
        gq = [None] * (mi + 1)
        for jm in range(mi - 1, -1, -1):
            gq[jm] = g_mid[jm] if gq[jm + 1] is None else gq[jm + 1] * g_mid[jm]
        gqb = [None if r is None else bc(r) for r in gq]
        gk = None
        for m2 in range(mi + 1, n_mid):
            gk = g_mid[m2] if gk is None else gk * g_mid[m2]
        gkb = None if gk is None else bc(gk)
        for i in range(n_sub):
            j = mi * n_sub + i
            if i < n_sub - 1:
                k2[i][j] = ke[j]
            m = qe[j]
            for jj in range(i, 0, -1):
                q2[jj - 1][j] = m
                if jj > 1:
                    m = m * bc(g[mi * n_sub + jj - 1])
            qm = qe[j] if pre[mi][i] is None else qe[j] * bc(pre[mi][i])
            km = ke[j] if suf[mi][i] is None else ke[j] * bc(suf[mi][i])
            if mi < n_mid - 1:
                k1[mi][j] = km
            for jm in range(1, mi + 1):
                q1[jm - 1][j] = qm if gqb[jm] is None else qm * gqb[jm]
            q0[j] = qm if gqb[0] is None else qm * gqb[0]
            kend[j] = km if gkb is None else km * gkb
    g_all = g_mid[0]
    for mi in range(1, n_mid):
        g_all = g_all * g_mid[mi]

    def rows(pieces):
        return jnp.concatenate(pieces, axis=0)

    def slots(groups):
        return jnp.concatenate([rows(p) for p in groups], axis=1).astype(BF16)

    s1 = _dot_nt(slots(q1), slots(k1))
    s2 = _dot_nt(slots(q2), slots(k2))
    scores = (s1 + jnp.where(same_mid, s2, 0.0)).astype(BF16)
    o = _dot_nt(rows(q0).astype(BF16), st.astype(BF16)) + jnp.dot(scores, vb, preferred_element_type=F32)

    w, vd = k3, v3
    band = jnp.sum(q3 * w, axis=-1, keepdims=True) * vd
    for d in range(1, HG_SUB):
        w = pltpu.roll(w, 1, 1) * f3
        vd = pltpu.roll(vd, 1, 1)
        sd = jnp.sum(q3 * w, axis=-1, keepdims=True)
        band = band + jnp.where(sub >= d, sd, 0.0) * vd
    o = o + band.reshape(C, HEAD_DIM)

    st_new = st * g_all + _dot_tn(vb, rows(kend).astype(BF16))

    ms = jnp.mean(o * o, axis=-1, keepdims=True)
    y = o * lax.rsqrt(ms + EPS) * ng * _silu(gr)
    return y, st_new


def _hgrn_kernel(q_ref, f_ref, i_ref, g_ref, lb_ref, ng_ref, o_ref, st_ref):
    C = HG_CHUNK
    n_heads = st_ref.shape[0]
    n_chunks = q_ref.shape[0] // C

    @pl.when(pl.program_id(1) == 0)
    def _():
        st_ref[...] = jnp.zeros_like(st_ref)

    r_i = lax.broadcasted_iota(jnp.int32, (C, C), 0)
    c_i = lax.broadcasted_iota(jnp.int32, (C, C), 1)
    same_mid = (r_i // HG_MID) == (c_i // HG_MID)
    sub = lax.broadcasted_iota(jnp.int32, (C // HG_SUB, HG_SUB, 1), 1)
    ng = ng_ref[...]

    def chunk(ci, carry):
        rows = pl.ds(pl.multiple_of(ci * C, C), C)
        for h in range(n_heads):
            sl = pl.ds(h * HEAD_DIM, HEAD_DIM)
            y, st_new = _hgrn_head(
                q_ref[rows, sl].astype(F32), f_ref[rows, sl].astype(F32), i_ref[rows, sl].astype(F32),
                g_ref[rows, sl].astype(F32), lb_ref[:, sl], ng, st_ref[h], sub, same_mid)
            st_ref[h] = st_new
            o_ref[rows, sl] = y.astype(o_ref.dtype)
        return carry

    lax.fori_loop(0, n_chunks, chunk, 0)


def _hgrn(p, lb, norm_g, *, batch, seq, key_width, val_width):
    assert key_width == val_width
    N = p.shape[0]
    tt = min(HG_ROWS, seq)
    assert tt % HG_CHUNK == 0
    n_heads = key_width // HEAD_DIM
    per_b = seq // tt
    col = lambda g: (lambda bi, t: (bi * per_b + t, g))
    return pl.pallas_call(
        _hgrn_kernel,
        grid=(batch, per_b),
        in_specs=[
            pl.BlockSpec((tt, key_width), col(0)),
            pl.BlockSpec((tt, key_width), col(1)),
            pl.BlockSpec((tt, key_width), col(2)),
            pl.BlockSpec((tt, key_width), col(3)),
            pl.BlockSpec((1, key_width), lambda bi, t: (0, 0)),
            pl.BlockSpec((1, HEAD_DIM), lambda bi, t: (0, 0)),
        ],
        out_specs=pl.BlockSpec((tt, val_width), lambda bi, t: (bi * per_b + t, 0)),
        out_shape=jax.ShapeDtypeStruct((N, val_width), BF16),
        scratch_shapes=[pltpu.VMEM((n_heads, HEAD_DIM, HEAD_DIM), F32)],
        compiler_params=_cparams(("parallel", "arbitrary")),
        name="hgrn2_mixer",
    )(p, p, p, p, lb.reshape(1, key_width), norm_g.reshape(1, HEAD_DIM))


def _sgu_kernel(u_ref, v_ref, lng_ref, lnb_ref, w_ref, bs_ref, o_ref):
    C = SGU_CHUNK
    n_groups = w_ref.shape[0]
    r_i = lax.broadcasted_iota(jnp.int32, (C, C), 0)
    c_i = lax.broadcasted_iota(jnp.int32, (C, C), 1)
    tril = jnp.where(r_i >= c_i, 1.0, 0.0)
    for ci in range(u_ref.shape[0] // C):
        rows = pl.ds(ci * C, C)
        vg = _gelu(v_ref[rows, :].astype(F32))
        mu = jnp.mean(vg, axis=-1, keepdims=True)
        vc = vg - mu
        var = jnp.mean(vc * vc, axis=-1, keepdims=True)
        vn = (vc * lax.rsqrt(var + EPS) * lng_ref[...] + lnb_ref[...]).astype(BF16)
        for g in range(n_groups):
            cols = pl.ds(g * GROUP_DIM, GROUP_DIM)
            w = (w_ref[g] * tril).astype(BF16)
            s = jnp.dot(w, vn[:, g * GROUP_DIM:(g + 1) * GROUP_DIM], preferred_element_type=F32)
            s = s + bs_ref[:, g:g + 1]
            u = _gelu(u_ref[rows, cols].astype(F32))
            o_ref[rows, cols] = (u * s).astype(o_ref.dtype)


def _sgu(p, ln_g, ln_b, w_s, b_s, *, seq, col_u, width):
    N = p.shape[0]
    G, C, _ = w_s.shape
    assert C == SGU_CHUNK and G * GROUP_DIM == width and col_u % width == 0
    tt = min(512, seq)
    cu = col_u // width
    return pl.pallas_call(
        _sgu_kernel,
        grid=(N // tt,),
        in_specs=[
            pl.BlockSpec((tt, width), lambda i: (i, cu)),
            pl.BlockSpec((tt, width), lambda i: (i, cu + 1)),
            pl.BlockSpec((1, width), lambda i: (0, 0)),
            pl.BlockSpec((1, width), lambda i: (0, 0)),
            pl.BlockSpec((G, C, C), lambda i: (0, 0, 0)),
            pl.BlockSpec((C, G), lambda i: (0, 0)),
        ],
        out_specs=pl.BlockSpec((tt, width), lambda i: (i, 0)),
        out_shape=jax.ShapeDtypeStruct((N, width), BF16),
        compiler_params=_cparams(("parallel",)),
        name="gmlp_spatial_gating",
    )(p, p, ln_g.reshape(1, width), ln_b.reshape(1, width), w_s, b_s.T)


def _merge_kernel(oa_ref, ob_ref, ga_ref, gb_ref, wa_ref, wb_ref, wo_ref, x_ref, gt_ref, x1_ref):
    @pl.when(pl.program_id(1) == 0)
    def _():
        x1_ref[...] = x_ref[...]

    ya = jnp.dot(oa_ref[...], wa_ref[...], preferred_element_type=F32)
    yb = jnp.dot(ob_ref[...], wb_ref[...], preferred_element_type=F32)
    m = jax.nn.sigmoid(ga_ref[...].astype(F32)) * ya + jax.nn.sigmoid(gb_ref[...].astype(F32)) * yb
    x1_ref[...] += gt_ref[...] * jnp.dot(m.astype(BF16), wo_ref[...], preferred_element_type=F32)


def _merge(x, p, oa, ob, wa, wb, wo, l, mod, *, seq, col_ga):
    N, D = x.shape
    Wa = oa.shape[1]
    tk = wa.shape[3]
    tm = min(512, seq)
    assert col_ga % tk == 0 and D % tk == 0
    per_b = seq // tm
    cga = col_ga // tk
    cgb = (col_ga + D) // tk
    return pl.pallas_call(
        _merge_kernel,
        grid=(N // tm, D // tk),
        in_specs=[
            pl.BlockSpec((tm, Wa), lambda i, k: (i, 0)),
            pl.BlockSpec((tm, Wa), lambda i, k: (i, 0)),
            pl.BlockSpec((tm, tk), lambda i, k: (i, cga + k)),
            pl.BlockSpec((tm, tk), lambda i, k: (i, cgb + k)),
            pl.BlockSpec((None, None, Wa, tk), lambda i, k: (l, k, 0, 0)),
            pl.BlockSpec((None, None, Wa, tk), lambda i, k: (l, k, 0, 0)),
            pl.BlockSpec((None, tk, D), lambda i, k: (l, k, 0)),
            pl.BlockSpec((tm, D), lambda i, k: (i, 0)),
            pl.BlockSpec((None, 1, D), lambda i, k: ((i // per_b) * N_MOD + 2, 0, 0)),
        ],
        out_specs=pl.BlockSpec((tm, D), lambda i, k: (i, 0)),
        out_shape=jax.ShapeDtypeStruct((N, D), F32),
        compiler_params=_cparams(("parallel", "arbitrary")),
        name="merge_out_projection",
    )(oa, ob, p, p, wa, wb, wo, x, mod)


def _ffn_kernel(x_ref, g_ref, sh_ref, sc_ref, gt_ref, w1_ref, w2_ref, fg_ref, o_ref, h_ref, *, final_norm):
    kk = pl.program_id(1)

    @pl.when(kk == 0)
    def _():
        x = x_ref[...]
        h_ref[...] = _modnorm(x, g_ref[...], sh_ref[...], sc_ref[...]).astype(BF16)
        o_ref[...] = x

    hid = jnp.dot(h_ref[...], w1_ref[...], preferred_element_type=F32)
    act = jnp.square(jnp.maximum(hid, 0.0)).astype(BF16)
    o_ref[...] += gt_ref[...] * jnp.dot(act, w2_ref[...], preferred_element_type=F32)

    if final_norm:
        @pl.when(kk == pl.num_programs(1) - 1)
        def _():
            x2 = o_ref[...]
            ms = jnp.mean(x2 * x2, axis=-1, keepdims=True)
            o_ref[...] = x2 * lax.rsqrt(ms + EPS) * fg_ref[...]


def _ffn(x1, w1, w2, l, mod, norm2_g, final_g, *, seq, final_norm):
    N, D = x1.shape
    th = w1.shape[3]
    FF = w1.shape[1] * th
    tm = min(1024, seq)
    per_b = seq // tm
    modspec = lambda which: pl.BlockSpec((None, 1, D), lambda i, k: ((i // per_b) * N_MOD + which, 0, 0))
    return pl.pallas_call(
        functools.partial(_ffn_kernel, final_norm=final_norm),
        grid=(N // tm, FF // th),
        in_specs=[
            pl.BlockSpec((tm, D), lambda i, k: (i, 0)),
            pl.BlockSpec((1, D), lambda i, k: (0, 0)),
            modspec(3),
            modspec(4),
            modspec(5),
            pl.BlockSpec((None, None, D, th), lambda i, k: (l, k, 0, 0)),
            pl.BlockSpec((None, th, D), lambda i, k: (l, k, 0)),
            pl.BlockSpec((1, D), lambda i, k: (0, 0)),
        ],
        out_specs=pl.BlockSpec((tm, D), lambda i, k: (i, 0)),
        out_shape=jax.ShapeDtypeStruct((N, D), F32),
        scratch_shapes=[pltpu.VMEM((tm, D), BF16)],
        compiler_params=_cparams(("parallel", "arbitrary")),
        name="relu2_mlp",
    )(x1, norm2_g.reshape(1, D), mod, mod, mod, w1, w2, final_g.reshape(1, D))


def kernel(x, c, w_ada, b_ada, norm1_g, norm2_g, w_in, hgrn_lower_bounds, hgrn_norm_g, ln_v_g, ln_v_b,
           w_spatial, b_spatial, w_proj_a, w_proj_b, w_out, w_ff_in, w_ff_out, final_norm_g):
    B, T, D = x.shape
    L = w_ada.shape[0]
    N = B * T
    key_width = hgrn_lower_bounds.shape[1]
    val_width = w_proj_a.shape[1]
    sgu_width = w_proj_b.shape[1]
    col_u = 2 * key_width + 2 * val_width
    col_ga = col_u + 2 * sgu_width

    mod_all = _modulation(c, w_ada, b_ada)
    lbs = _lower_bounds(hgrn_lower_bounds)
    w_in_b = _chunk_cols(w_in, INPROJ_TN)
    wa_b, wb_b = _chunk_cols(w_proj_a, MERGE_TK), _chunk_cols(w_proj_b, MERGE_TK)
    w1_b = _chunk_cols(w_ff_in, FFN_TH)
    wo_b, w2_b = w_out.astype(BF16), w_ff_out.astype(BF16)

    xf = x.reshape(N, D)
    for l in range(L):
        mod = mod_all[l].reshape(B * N_MOD, 1, D)
        p = _inproj(xf, mod, norm1_g[l], w_in_b, l, seq=T)
        oa = _hgrn(p, lbs[l], hgrn_norm_g[l], batch=B, seq=T, key_width=key_width, val_width=val_width)
        ob = _sgu(p, ln_v_g[l], ln_v_b[l], w_spatial[l], b_spatial[l], seq=T, col_u=col_u, width=sgu_width)
        x1 = _merge(xf, p, oa, ob, wa_b, wb_b, wo_b, l, mod, seq=T, col_ga=col_ga)
        xf = _ffn(x1, w1_b, w2_b, l, mod, norm2_g[l], final_norm_g, seq=T, final_norm=(l == L - 1))
    return xf.reshape(B, T, D)
```

```python
import functools

import jax
import jax.numpy as jnp
from jax import lax
from jax.experimental import pallas as pl
from jax.experimental.pallas import tpu as pltpu

F32 = jnp.float32
BF16 = jnp.bfloat16

EPS = 1e-6
MIN_FORGET = 1e-20
N_MOD = 6
HEAD_DIM = 128
GROUP_DIM = 128
SGU_CHUNK = 128
HG_CHUNK = 128
HG_MID = 32
HG_SUB = 8
HG_ROWS = 512
VMEM_LIMIT = 56 * 1024 * 1024
INPROJ_TN = 2048
MERGE_TK = 1024
FFN_TH = 512


def _chunk_cols(w, t):
    L, R, C = w.shape
    t = min(t, C)
    return w.reshape(L, R, C // t, t).transpose(0, 2, 1, 3).astype(BF16)


def _cparams(sem):
    return pltpu.CompilerParams(dimension_semantics=sem, vmem_limit_bytes=VMEM_LIMIT)


def _silu(x):
    return x * jax.nn.sigmoid(x)


def _gelu(x):
    return 0.5 * x * (1.0 + lax.erf(x * (0.5 ** 0.5)))


def _modnorm(x, g, shift, scale):
    ms = jnp.mean(x * x, axis=-1, keepdims=True)
    y = x * lax.rsqrt(ms + EPS) * g
    return y * (1.0 + scale) + shift


def _mod_kernel(c_ref, w_ref, b_ref, o_ref):
    c = c_ref[...]
    ca = _silu(c).astype(BF16)
    o_ref[...] = jnp.dot(ca, w_ref[...].astype(BF16), preferred_element_type=F32) + b_ref[...]


def _modulation(c, w_ada, b_ada):
    L, D, W = w_ada.shape
    B = c.shape[0]
    tn = min(1024, W)
    return pl.pallas_call(
        _mod_kernel,
        grid=(L, W // tn),
        in_specs=[
            pl.BlockSpec((B, D), lambda l, j: (0, 0)),
            pl.BlockSpec((None, D, tn), lambda l, j: (l, 0, j)),
            pl.BlockSpec((None, 1, tn), lambda l, j: (l, 0, j)),
        ],
        out_specs=pl.BlockSpec((None, B, tn), lambda l, j: (l, 0, j)),
        out_shape=jax.ShapeDtypeStruct((L, B, W), F32),
        compiler_params=_cparams(("arbitrary", "arbitrary")),
        name="adaln_modulation",
    )(c, w_ada, b_ada.reshape(L, 1, W))


def _lower_bounds_kernel(lb_ref, o_ref):
    x = lb_ref[...]
    m = jnp.max(x, axis=0, keepdims=True)
    e = jnp.exp(x - m)
    p = e / jnp.sum(e, axis=0, keepdims=True)
    L = x.shape[0]
    acc = jnp.zeros_like(p[0:1])
    rows = []
    for l in range(L):
        acc = acc + p[l:l + 1]
        rows.append(acc - p[0:1])
    o_ref[...] = jnp.concatenate(rows, axis=0)


def _lower_bounds(lb):
    return pl.pallas_call(
        _lower_bounds_kernel,
        out_shape=jax.ShapeDtypeStruct(lb.shape, F32),
        name="hgrn_lower_bounds",
    )(lb)


def _inproj_kernel(x_ref, sh_ref, sc_ref, g_ref, w_ref, lb_ref, lng_ref, lnb_ref, p_ref, h_ref):
    j = pl.program_id(1)
    half = w_ref.shape[1] // 2

    def project():
        return jnp.dot(h_ref[...], w_ref[...], preferred_element_type=F32)

    @pl.when(j == 0)
    def _():
        h_ref[...] = _modnorm(x_ref[...], g_ref[...], sh_ref[...], sc_ref[...]).astype(BF16)
        acc = project()
        lb = lb_ref[...]
        f = lb + (1.0 - lb) * jax.nn.sigmoid(acc[:, half:])
        p_ref[:, :half] = _silu(acc[:, :half]).astype(p_ref.dtype)
        p_ref[:, half:] = jnp.log(jnp.maximum(f, MIN_FORGET)).astype(p_ref.dtype)

    @pl.when(j == 1)
    def _():
        acc = project()
        p_ref[:, :half] = acc[:, :half].astype(p_ref.dtype)
        p_ref[:, half:] = _silu(acc[:, half:]).astype(p_ref.dtype)

    @pl.when(j == 2)
    def _():
        acc = project()
        p_ref[:, :half] = _gelu(acc[:, :half]).astype(p_ref.dtype)
        vg = _gelu(acc[:, half:])
        mu = jnp.mean(vg, axis=-1, keepdims=True)
        vc = vg - mu
        var = jnp.mean(vc * vc, axis=-1, keepdims=True)
        p_ref[:, half:] = (vc * lax.rsqrt(var + EPS) * lng_ref[...] + lnb_ref[...]).astype(p_ref.dtype)

    @pl.when(j >= 3)
    def _():
        p_ref[...] = project().astype(p_ref.dtype)


def _inproj(x, mod, norm_g, w, l, lb, ln_g, ln_b, *, seq):
    N, D = x.shape
    tn = w.shape[3]
    W = w.shape[1] * tn
    half = tn // 2
    assert lb.shape[-1] == half and ln_g.shape[-1] == half and W == 3 * tn + 2 * D and D == tn
    tm = min(1024, seq)
    per_b = seq // tm
    return pl.pallas_call(
        _inproj_kernel,
        grid=(N // tm, W // tn),
        in_specs=[
            pl.BlockSpec((tm, D), lambda i, j: (i, 0)),
            pl.BlockSpec((None, 1, D), lambda i, j: ((i // per_b) * N_MOD + 0, 0, 0)),
            pl.BlockSpec((None, 1, D), lambda i, j: ((i // per_b) * N_MOD + 1, 0, 0)),
            pl.BlockSpec((1, D), lambda i, j: (0, 0)),
            pl.BlockSpec((None, None, D, tn), lambda i, j: (l, j, 0, 0)),
            pl.BlockSpec((1, half), lambda i, j: (0, 0)),
            pl.BlockSpec((1, half), lambda i, j: (0, 0)),
            pl.BlockSpec((1, half), lambda i, j: (0, 0)),
        ],
        out_specs=pl.BlockSpec((tm, tn), lambda i, j: (i, j)),
        out_shape=jax.ShapeDtypeStruct((N, W), BF16),
        scratch_shapes=[pltpu.VMEM((tm, D), BF16)],
        compiler_params=_cparams(("parallel", "arbitrary")),
        name="in_projection",
    )(x, mod, mod, norm_g.reshape(1, D), w, lb.reshape(1, half), ln_g.reshape(1, half), ln_b.reshape(1, half))


def _dot_nt(a, b):
    return lax.dot_general(a, b, (((1,), (1,)), ((), ())), preferred_element_type=F32)


def _dot_tn(a, b):
    return lax.dot_general(a, b, (((0,), (0,)), ((), ())), preferred_element_type=F32)


def _hgrn_head(qs, lf, vr, gate, ng, st, sub, same_mid, same_sub):
    C = HG_CHUNK
    nb = C // HG_SUB
    n_mid = C // HG_MID
    n_sub = HG_MID // HG_SUB
    hb = HG_SUB // 2
    shp3 = (nb, HG_SUB, HEAD_DIM)

    q3 = qs.reshape(shp3)
    v3 = vr.reshape(shp3)
    vb = vr.astype(BF16)
    hi = sub >= hb
    sub_h = jnp.where(hi, sub - hb, sub)

    a = lf.reshape(shp3)
    f3 = jnp.exp(a)
    k3 = 1.0 - f3
    for sh in (1, 2):
        a = a + jnp.where(sub_h >= sh, pltpu.roll(a, sh, 1), 0.0)
    e4 = jnp.exp(a)
    tot = jnp.where(hi, a[:, HG_SUB - 1:HG_SUB, :], a[:, hb - 1:hb, :])
    qe4 = q3 * e4
    ke4 = k3 * jnp.exp(tot - a)
    g_lo, g_hi = e4[:, hb - 1:hb, :], e4[:, HG_SUB - 1:HG_SUB, :]
    s3 = _dot_nt(jnp.where(hi, qe4, 0.0).reshape(C, HEAD_DIM).astype(BF16),
                 jnp.where(hi, 0.0, ke4).reshape(C, HEAD_DIM).astype(BF16))
    qe = jnp.where(hi, qe4 * g_lo, qe4)
    ke = jnp.where(hi, ke4, ke4 * g_hi)
    g8 = g_lo * g_hi
    g = [g8[j] for j in range(nb)]

    def bc(r):
        return jnp.broadcast_to(r, (HG_SUB, HEAD_DIM))

    pre, suf, g_mid = [], [], []
    for mi in range(n_mid):
        gs = g[mi * n_sub:(mi + 1) * n_sub]
        p = [None]
        for i in range(1, n_sub):
            p.append(gs[i - 1] if p[-1] is None else p[-1] * gs[i - 1])
        s = [None] * n_sub
        for i in range(n_sub - 2, -1, -1):
            s[i] = gs[i + 1] if s[i + 1] is None else s[i + 1] * gs[i + 1]
        pre.append(p)
        suf.append(s)
        g_mid.append(p[-1] * gs[-1])

    zero = jnp.zeros((HG_SUB, HEAD_DIM), F32)
    q2 = [[zero] * nb for _ in range(n_sub - 1)]
    k2 = [[zero] * nb for _ in range(n_sub - 1)]
    q1 = [[zero] * nb for _ in range(n_mid - 1)]
    k1 = [[zero] * nb for _ in range(n_mid - 1)]
    q0, kend = [None] * nb, [None] * nb
    for mi in range(n_mid):
        gq = [None] * (mi + 1)
        for jm in range(mi - 1, -1, -1):
            gq[jm] = g_mid[jm] if gq[jm + 1] is None else gq[jm + 1] * g_mid[jm]
        gqb = [None if r is None else bc(r) for r in gq]
        gk = None
        for m2 in range(mi + 1, n_mid):
            gk = g_mid[m2] if gk is None else gk * g_mid[m2]
        gkb = None if gk is None else bc(gk)
        for i in range(n_sub):
            j = mi * n_sub + i
            if i < n_sub - 1:
                k2[i][j] = ke[j]
            m = qe[j]
            for jj in range(i, 0, -1):
                q2[jj - 1][j] = m
                if jj > 1:
                    m = m * bc(g[mi * n_sub + jj - 1])
            qm = qe[j] if pre[mi][i] is None else qe[j] * bc(pre[mi][i])
            km = ke[j] if suf[mi][i] is None else ke[j] * bc(suf[mi][i])
            if mi < n_mid - 1:
                k1[mi][j] = km
            for jm in range(1, mi + 1):
                q1[jm - 1][j] = qm if gqb[jm] is None else qm * gqb[jm]
            q0[j] = qm if gqb[0] is None else qm * gqb[0]
            kend[j] = km if gkb is None else km * gkb
    g_all = g_mid[0]
    for mi in range(1, n_mid):
        g_all = g_all * g_mid[mi]

    def rows(pieces):
        return jnp.concatenate(pieces, axis=0)

    def slots(groups):
        return jnp.concatenate([rows(p) for p in groups], axis=1).astype(BF16)

    s1 = _dot_nt(slots(q1), slots(k1))
    s2 = _dot_nt(slots(q2), slots(k2))
    scores = (s1 + jnp.where(same_mid, s2 + jnp.where(same_sub, s3, 0.0), 0.0)).astype(BF16)
    o = _dot_nt(rows(q0).astype(BF16), st.astype(BF16)) + jnp.dot(scores, vb, preferred_element_type=F32)

    w, vd = k3, v3
    band = jnp.sum(q3 * w, axis=-1, keepdims=True) * vd
    for d in range(1, hb):
        w = pltpu.roll(w, 1, 1) * f3
        vd = pltpu.roll(vd, 1, 1)
        sd = jnp.sum(q3 * w, axis=-1, keepdims=True)
        band = band + jnp.where(sub_h >= d, sd, 0.0) * vd
    o = o + band.reshape(C, HEAD_DIM)

    st_new = st * g_all + _dot_tn(vb, rows(kend).astype(BF16))

    ms = jnp.mean(o * o, axis=-1, keepdims=True)
    y = o * lax.rsqrt(ms + EPS) * ng * gate
    return y, st_new


def _hgrn_kernel(q_ref, lf_ref, i_ref, g_ref, ng_ref, o_ref, st_ref):
    C = HG_CHUNK
    n_heads = st_ref.shape[0]
    n_chunks = q_ref.shape[0] // C

    @pl.when(pl.program_id(1) == 0)
    def _():
        st_ref[...] = jnp.zeros_like(st_ref)

    r_i = lax.broadcasted_iota(jnp.int32, (C, C), 0)
    c_i = lax.broadcasted_iota(jnp.int32, (C, C), 1)
    same_mid = (r_i // HG_MID) == (c_i // HG_MID)
    same_sub = (r_i // HG_SUB) == (c_i // HG_SUB)
    sub = lax.broadcasted_iota(jnp.int32, (C // HG_SUB, HG_SUB, 1), 1)
    ng = ng_ref[...]

    def chunk(ci, carry):
        rows = pl.ds(pl.multiple_of(ci * C, C), C)
        for h in range(n_heads):
            sl = pl.ds(h * HEAD_DIM, HEAD_DIM)
            y, st_new = _hgrn_head(
                q_ref[rows, sl].astype(F32), lf_ref[rows, sl].astype(F32), i_ref[rows, sl].astype(F32),
                g_ref[rows, sl].astype(F32), ng, st_ref[h], sub, same_mid, same_sub)
            st_ref[h] = st_new
            o_ref[rows, sl] = y.astype(o_ref.dtype)
        return carry

    lax.fori_loop(0, n_chunks, chunk, 0)


def _hgrn(p, norm_g, *, batch, seq, key_width, val_width):
    assert key_width == val_width
    N = p.shape[0]
    tt = min(HG_ROWS, seq)
    assert tt % HG_CHUNK == 0
    n_heads = key_width // HEAD_DIM
    per_b = seq // tt
    col = lambda g: (lambda bi, t: (bi * per_b + t, g))
    return pl.pallas_call(
        _hgrn_kernel,
        grid=(batch, per_b),
        in_specs=[
            pl.BlockSpec((tt, key_width), col(0)),
            pl.BlockSpec((tt, key_width), col(1)),
            pl.BlockSpec((tt, key_width), col(2)),
            pl.BlockSpec((tt, key_width), col(3)),
            pl.BlockSpec((1, HEAD_DIM), lambda bi, t: (0, 0)),
        ],
        out_specs=pl.BlockSpec((tt, val_width), lambda bi, t: (bi * per_b + t, 0)),
        out_shape=jax.ShapeDtypeStruct((N, val_width), BF16),
        scratch_shapes=[pltpu.VMEM((n_heads, HEAD_DIM, HEAD_DIM), F32)],
        compiler_params=_cparams(("parallel", "arbitrary")),
        name="hgrn2_mixer",
    )(p, p, p, p, norm_g.reshape(1, HEAD_DIM))


def _sgu_kernel(u_ref, v_ref, w_ref, bs_ref, o_ref):
    C = SGU_CHUNK
    n_groups = w_ref.shape[0]
    r_i = lax.broadcasted_iota(jnp.int32, (C, C), 0)
    c_i = lax.broadcasted_iota(jnp.int32, (C, C), 1)
    tril = jnp.where(r_i >= c_i, 1.0, 0.0)
    ws = [(w_ref[g] * tril).astype(BF16) for g in range(n_groups)]
    for ci in range(u_ref.shape[0] // C):
        rows = pl.ds(ci * C, C)
        for g in range(n_groups):
            cols = pl.ds(g * GROUP_DIM, GROUP_DIM)
            s = jnp.dot(ws[g], v_ref[rows, cols], preferred_element_type=F32) + bs_ref[:, g:g + 1]
            o_ref[rows, cols] = (u_ref[rows, cols].astype(F32) * s).astype(o_ref.dtype)


def _sgu(p, w_s, b_s, *, seq, col_u, width):
    N = p.shape[0]
    G, C, _ = w_s.shape
    assert C == SGU_CHUNK and G * GROUP_DIM == width and col_u % width == 0
    tt = min(1024, seq)
    cu = col_u // width
    return pl.pallas_call(
        _sgu_kernel,
        grid=(N // tt,),
        in_specs=[
            pl.BlockSpec((tt, width), lambda i: (i, cu)),
            pl.BlockSpec((tt, width), lambda i: (i, cu + 1)),
            pl.BlockSpec((G, C, C), lambda i: (0, 0, 0)),
            pl.BlockSpec((C, G), lambda i: (0, 0)),
        ],
        out_specs=pl.BlockSpec((tt, width), lambda i: (i, 0)),
        out_shape=jax.ShapeDtypeStruct((N, width), BF16),
        compiler_params=_cparams(("parallel",)),
        name="gmlp_spatial_gating",
    )(p, p, w_s, b_s.T)


def _merge_kernel(oa_ref, ob_ref, ga_ref, gb_ref, wa_ref, wb_ref, wo_ref, x_ref, gt_ref, x1_ref):
    def gated_chunk():
        ya = jnp.dot(oa_ref[...], wa_ref[...], preferred_element_type=F32)
        yb = jnp.dot(ob_ref[...], wb_ref[...], preferred_element_type=F32)
        m = jax.nn.sigmoid(ga_ref[...].astype(F32)) * ya + jax.nn.sigmoid(gb_ref[...].astype(F32)) * yb
        return gt_ref[...] * jnp.dot(m.astype(BF16), wo_ref[...], preferred_element_type=F32)

    @pl.when(pl.program_id(1) == 0)
    def _():
        x1_ref[...] = x_ref[...] + gated_chunk()

    @pl.when(pl.program_id(1) > 0)
    def _():
        x1_ref[...] += gated_chunk()


def _merge(x, p, oa, ob, wa, wb, wo, l, mod, *, seq, col_ga):
    N, D = x.shape
    Wa = oa.shape[1]
    tk = wa.shape[3]
    tm = min(512, seq)
    assert col_ga % tk == 0 and D % tk == 0
    per_b = seq // tm
    cga = col_ga // tk
    cgb = (col_ga + D) // tk
    return pl.pallas_call(
        _merge_kernel,
        grid=(N // tm, D // tk),
        in_specs=[
            pl.BlockSpec((tm, Wa), lambda i, k: (i, 0)),
            pl.BlockSpec((tm, Wa), lambda i, k: (i, 0)),
            pl.BlockSpec((tm, tk), lambda i, k: (i, cga + k)),
            pl.BlockSpec((tm, tk), lambda i, k: (i, cgb + k)),
            pl.BlockSpec((None, None, Wa, tk), lambda i, k: (l, k, 0, 0)),
            pl.BlockSpec((None, None, Wa, tk), lambda i, k: (l, k, 0, 0)),
            pl.BlockSpec((None, tk, D), lambda i, k: (l, k, 0)),
            pl.BlockSpec((tm, D), lambda i, k: (i, 0)),
            pl.BlockSpec((None, 1, D), lambda i, k: ((i // per_b) * N_MOD + 2, 0, 0)),
        ],
        out_specs=pl.BlockSpec((tm, D), lambda i, k: (i, 0)),
        out_shape=jax.ShapeDtypeStruct((N, D), F32),
        compiler_params=_cparams(("parallel", "arbitrary")),
        name="merge_out_projection",
    )(oa, ob, p, p, wa, wb, wo, x, mod)


def _ffn_kernel(x_ref, g_ref, sh_ref, sc_ref, gt_ref, w1_ref, w2_ref, fg_ref, o_ref, h_ref, *, final_norm):
    kk = pl.program_id(1)

    def gated_chunk(h):
        hid = jnp.dot(h, w1_ref[...], preferred_element_type=F32)
        act = jnp.square(jnp.maximum(hid, 0.0)).astype(BF16)
        return gt_ref[...] * jnp.dot(act, w2_ref[...], preferred_element_type=F32)

    @pl.when(kk == 0)
    def _():
        x = x_ref[...]
        h = _modnorm(x, g_ref[...], sh_ref[...], sc_ref[...]).astype(BF16)
        h_ref[...] = h
        o_ref[...] = x + gated_chunk(h)

    @pl.when(kk > 0)
    def _():
        o_ref[...] += gated_chunk(h_ref[...])

    if final_norm:
        @pl.when(kk == pl.num_programs(1) - 1)
        def _():
            x2 = o_ref[...]
            ms = jnp.mean(x2 * x2, axis=-1, keepdims=True)
            o_ref[...] = x2 * lax.rsqrt(ms + EPS) * fg_ref[...]


def _ffn(x1, w1, w2, l, mod, norm2_g, final_g, *, seq, final_norm):
    N, D = x1.shape
    th = w1.shape[3]
    FF = w1.shape[1] * th
    tm = min(1024, seq)
    per_b = seq // tm
    modspec = lambda which: pl.BlockSpec((None, 1, D), lambda i, k: ((i // per_b) * N_MOD + which, 0, 0))
    return pl.pallas_call(
        functools.partial(_ffn_kernel, final_norm=final_norm),
        grid=(N // tm, FF // th),
        in_specs=[
            pl.BlockSpec((tm, D), lambda i, k: (i, 0)),
            pl.BlockSpec((1, D), lambda i, k: (0, 0)),
            modspec(3),
            modspec(4),
            modspec(5),
            pl.BlockSpec((None, None, D, th), lambda i, k: (l, k, 0, 0)),
            pl.BlockSpec((None, th, D), lambda i, k: (l, k, 0)),
            pl.BlockSpec((1, D), lambda i, k: (0, 0)),
        ],
        out_specs=pl.BlockSpec((tm, D), lambda i, k: (i, 0)),
        out_shape=jax.ShapeDtypeStruct((N, D), F32),
        scratch_shapes=[pltpu.VMEM((tm, D), BF16)],
        compiler_params=_cparams(("parallel", "arbitrary")),
        name="relu2_mlp",
    )(x1, norm2_g.reshape(1, D), mod, mod, mod, w1, w2, final_g.reshape(1, D))


def kernel(x, c, w_ada, b_ada, norm1_g, norm2_g, w_in, hgrn_lower_bounds, hgrn_norm_g, ln_v_g, ln_v_b,
           w_spatial, b_spatial, w_proj_a, w_proj_b, w_out, w_ff_in, w_ff_out, final_norm_g):
    B, T, D = x.shape
    L = w_ada.shape[0]
    N = B * T
    key_width = hgrn_lower_bounds.shape[1]
    val_width = w_proj_a.shape[1]
    sgu_width = w_proj_b.shape[1]
    col_u = 2 * key_width + 2 * val_width
    col_ga = col_u + 2 * sgu_width

    mod_all = _modulation(c, w_ada, b_ada)
    lbs = _lower_bounds(hgrn_lower_bounds)
    w_in_b = _chunk_cols(w_in, INPROJ_TN)
    wa_b, wb_b = _chunk_cols(w_proj_a, MERGE_TK), _chunk_cols(w_proj_b, MERGE_TK)
    w1_b = _chunk_cols(w_ff_in, FFN_TH)
    wo_b, w2_b = w_out.astype(BF16), w_ff_out.astype(BF16)

    xf = x.reshape(N, D)
    for l in range(L):
        mod = mod_all[l].reshape(B * N_MOD, 1, D)
        p = _inproj(xf, mod, norm1_g[l], w_in_b, l, lbs[l], ln_v_g[l], ln_v_b[l], seq=T)
        oa = _hgrn(p, hgrn_norm_g[l], batch=B, seq=T, key_width=key_width, val_width=val_width)
        ob = _sgu(p, w_spatial[l], b_spatial[l], seq=T, col_u=col_u, width=sgu_width)
        x1 = _merge(xf, p, oa, ob, wa_b, wb_b, wo_b, l, mod, seq=T, col_ga=col_ga)
        xf = _ffn(x1, w1_b, w2_b, l, mod, norm2_g[l], final_norm_g, seq=T, final_norm=(l == L - 1))
    return xf.reshape(B, T, D)
```

```python
import functools

import jax
import jax.numpy as jnp
from jax import lax
from jax.experimental import pallas as pl
from jax.experimental.pallas import tpu as pltpu

F32 = jnp.float32
BF16 = jnp.bfloat16

EPS = 1e-6
MIN_FORGET = 1e-20
N_MOD = 6
HEAD_DIM = 128
GROUP_DIM = 128
SGU_CHUNK = 128
HG_CHUNK = 128
HG_MID = 32
HG_SUB = 8
HG_ROWS = 512
VMEM_LIMIT = 60 * 1024 * 1024
INPROJ_TN = 2048
MERGE_TK = 1024
FFN_TH = 1024


def _chunk_cols(w, t):
    L, R, C = w.shape
    t = min(t, C)
    return w.reshape(L, R, C // t, t).transpose(0, 2, 1, 3).astype(BF16)


def _cparams(sem):
    return pltpu.CompilerParams(dimension_semantics=sem, vmem_limit_bytes=VMEM_LIMIT)


def _silu(x):
    return x * jax.nn.sigmoid(x)


def _gelu(x):
    return 0.5 * x * (1.0 + lax.erf(x * (0.5 ** 0.5)))


def _modnorm(x, g, shift, scale):
    ms = jnp.mean(x * x, axis=-1, keepdims=True)
    y = x * lax.rsqrt(ms + EPS) * g
    return y * (1.0 + scale) + shift


def _mod_kernel(c_ref, w_ref, b_ref, o_ref):
    c = c_ref[...]
    ca = _silu(c).astype(BF16)
    o_ref[...] = jnp.dot(ca, w_ref[...].astype(BF16), preferred_element_type=F32) + b_ref[...]


def _modulation(c, w_ada, b_ada):
    L, D, W = w_ada.shape
    B = c.shape[0]
    tn = min(2048, W)
    return pl.pallas_call(
        _mod_kernel,
        grid=(L, W // tn),
        in_specs=[
            pl.BlockSpec((B, D), lambda l, j: (0, 0)),
            pl.BlockSpec((None, D, tn), lambda l, j: (l, 0, j)),
            pl.BlockSpec((None, 1, tn), lambda l, j: (l, 0, j)),
        ],
        out_specs=pl.BlockSpec((None, B, tn), lambda l, j: (l, 0, j)),
        out_shape=jax.ShapeDtypeStruct((L, B, W), F32),
        compiler_params=_cparams(("arbitrary", "arbitrary")),
        name="adaln_modulation",
    )(c, w_ada, b_ada.reshape(L, 1, W))


def _lower_bounds_kernel(lb_ref, o_ref):
    x = lb_ref[...]
    m = jnp.max(x, axis=0, keepdims=True)
    e = jnp.exp(x - m)
    p = e / jnp.sum(e, axis=0, keepdims=True)
    L = x.shape[0]
    acc = jnp.zeros_like(p[0:1])
    rows = []
    for l in range(L):
        acc = acc + p[l:l + 1]
        rows.append(acc - p[0:1])
    o_ref[...] = jnp.concatenate(rows, axis=0)


def _lower_bounds(lb):
    return pl.pallas_call(
        _lower_bounds_kernel,
        out_shape=jax.ShapeDtypeStruct(lb.shape, F32),
        name="hgrn_lower_bounds",
    )(lb)


def _inproj_kernel(x_ref, sh_ref, sc_ref, g_ref, w_ref, lb_ref, lng_ref, lnb_ref, p_ref, h_ref):
    j = pl.program_id(1)
    half = w_ref.shape[1] // 2

    def project():
        return jnp.dot(h_ref[...], w_ref[...], preferred_element_type=F32)

    @pl.when(j == 0)
    def _():
        h_ref[...] = _modnorm(x_ref[...], g_ref[...], sh_ref[...], sc_ref[...]).astype(BF16)
        acc = project()
        lb = lb_ref[...]
        f = lb + (1.0 - lb) * jax.nn.sigmoid(acc[:, half:])
        p_ref[:, :half] = _silu(acc[:, :half]).astype(p_ref.dtype)
        p_ref[:, half:] = jnp.log(jnp.maximum(f, MIN_FORGET)).astype(p_ref.dtype)

    @pl.when(j == 1)
    def _():
        acc = project()
        p_ref[:, :half] = acc[:, :half].astype(p_ref.dtype)
        p_ref[:, half:] = _silu(acc[:, half:]).astype(p_ref.dtype)

    @pl.when(j == 2)
    def _():
        acc = project()
        p_ref[:, :half] = _gelu(acc[:, :half]).astype(p_ref.dtype)
        vg = _gelu(acc[:, half:])
        mu = jnp.mean(vg, axis=-1, keepdims=True)
        vc = vg - mu
        var = jnp.mean(vc * vc, axis=-1, keepdims=True)
        p_ref[:, half:] = (vc * lax.rsqrt(var + EPS) * lng_ref[...] + lnb_ref[...]).astype(p_ref.dtype)

    @pl.when(j >= 3)
    def _():
        p_ref[...] = project().astype(p_ref.dtype)


def _inproj(x, mod, norm_g, w, l, lb, ln_g, ln_b, *, seq):
    N, D = x.shape
    tn = w.shape[3]
    W = w.shape[1] * tn
    half = tn // 2
    assert lb.shape[-1] == half and ln_g.shape[-1] == half and W == 3 * tn + 2 * D and D == tn
    tm = min(1024, seq)
    per_b = seq // tm
    return pl.pallas_call(
        _inproj_kernel,
        grid=(N // tm, W // tn),
        in_specs=[
            pl.BlockSpec((tm, D), lambda i, j: (i, 0)),
            pl.BlockSpec((None, 1, D), lambda i, j: ((i // per_b) * N_MOD + 0, 0, 0)),
            pl.BlockSpec((None, 1, D), lambda i, j: ((i // per_b) * N_MOD + 1, 0, 0)),
            pl.BlockSpec((1, D), lambda i, j: (0, 0)),
            pl.BlockSpec((None, None, D, tn), lambda i, j: (l, j, 0, 0)),
            pl.BlockSpec((1, half), lambda i, j: (0, 0)),
            pl.BlockSpec((1, half), lambda i, j: (0, 0)),
            pl.BlockSpec((1, half), lambda i, j: (0, 0)),
        ],
        out_specs=pl.BlockSpec((tm, tn), lambda i, j: (i, j)),
        out_shape=jax.ShapeDtypeStruct((N, W), BF16),
        scratch_shapes=[pltpu.VMEM((tm, D), BF16)],
        compiler_params=_cparams(("parallel", "arbitrary")),
        name="in_projection",
    )(x, mod, mod, norm_g.reshape(1, D), w, lb.reshape(1, half), ln_g.reshape(1, half), ln_b.reshape(1, half))


def _dot_nt(a, b):
    return lax.dot_general(a, b, (((1,), (1,)), ((), ())), preferred_element_type=F32)


def _dot_tn(a, b):
    return lax.dot_general(a, b, (((0,), (0,)), ((), ())), preferred_element_type=F32)


def _hgrn_head(qs, lf, vr, gate, ng, st, sub, same_mid, same_sub):
    C = HG_CHUNK
    nb = C // HG_SUB
    n_mid = C // HG_MID
    n_sub = HG_MID // HG_SUB
    hb = HG_SUB // 2
    shp3 = (nb, HG_SUB, HEAD_DIM)

    q3 = qs.reshape(shp3)
    v3 = vr.reshape(shp3)
    vb = vr.astype(BF16)
    hi = sub >= hb
    sub_h = jnp.where(hi, sub - hb, sub)

    a = lf.reshape(shp3)
    f3 = jnp.exp(a)
    k3 = 1.0 - f3
    for sh in (1, 2):
        a = a + jnp.where(sub_h >= sh, pltpu.roll(a, sh, 1), 0.0)
    e4 = jnp.exp(a)
    tot = jnp.where(hi, a[:, HG_SUB - 1:HG_SUB, :], a[:, hb - 1:hb, :])
    qe4 = q3 * e4
    ke4 = k3 * jnp.exp(tot - a)
    g_lo, g_hi = e4[:, hb - 1:hb, :], e4[:, HG_SUB - 1:HG_SUB, :]
    s3 = _dot_nt(jnp.where(hi, qe4, 0.0).reshape(C, HEAD_DIM).astype(BF16),
                 jnp.where(hi, 0.0, ke4).reshape(C, HEAD_DIM).astype(BF16))
    qe = jnp.where(hi, qe4 * g_lo, qe4)
    ke = jnp.where(hi, ke4, ke4 * g_hi)
    g8 = g_lo * g_hi
    g = [g8[j] for j in range(nb)]

    def bc(r):
        return jnp.broadcast_to(r, (HG_SUB, HEAD_DIM))

    pre, suf, g_mid = [], [], []
    for mi in range(n_mid):
        gs = g[mi * n_sub:(mi + 1) * n_sub]
        p = [None]
        for i in range(1, n_sub):
            p.append(gs[i - 1] if p[-1] is None else p[-1] * gs[i - 1])
        s = [None] * n_sub
        for i in range(n_sub - 2, -1, -1):
            s[i] = gs[i + 1] if s[i + 1] is None else s[i + 1] * gs[i + 1]
        pre.append(p)
        suf.append(s)
        g_mid.append(p[-1] * gs[-1])

    zero = jnp.zeros((HG_SUB, HEAD_DIM), F32)
    q2 = [[zero] * nb for _ in range(n_sub - 1)]
    k2 = [[zero] * nb for _ in range(n_sub - 1)]
    q1 = [[zero] * nb for _ in range(n_mid - 1)]
    k1 = [[zero] * nb for _ in range(n_mid - 1)]
    q0, kend = [None] * nb, [None] * nb
    for mi in range(n_mid):
        gq = [None] * (mi + 1)
        for jm in range(mi - 1, -1, -1):
            gq[jm] = g_mid[jm] if gq[jm + 1] is None else gq[jm + 1] * g_mid[jm]
        gqb = [None if r is None else bc(r) for r in gq]
        gk = None
        for m2 in range(mi + 1, n_mid):
            gk = g_mid[m2] if gk is None else gk * g_mid[m2]
        gkb = None if gk is None else bc(gk)
        for i in range(n_sub):
            j = mi * n_sub + i
            if i < n_sub - 1:
                k2[i][j] = ke[j]
            m = qe[j]
            for jj in range(i, 0, -1):
                q2[jj - 1][j] = m
                if jj > 1:
                    m = m * bc(g[mi * n_sub + jj - 1])
            qm = qe[j] if pre[mi][i] is None else qe[j] * bc(pre[mi][i])
            km = ke[j] if suf[mi][i] is None else ke[j] * bc(suf[mi][i])
            if mi < n_mid - 1:
                k1[mi][j] = km
            for jm in range(1, mi + 1):
                q1[jm - 1][j] = qm if gqb[jm] is None else qm * gqb[jm]
            q0[j] = qm if gqb[0] is None else qm * gqb[0]
            kend[j] = km if gkb is None else km * gkb
    g_all = g_mid[0]
    for mi in range(1, n_mid):
        g_all = g_all * g_mid[mi]

    def rows(pieces):
        return jnp.concatenate(pieces, axis=0)

    def slots(groups):
        return jnp.concatenate([rows(p) for p in groups], axis=1).astype(BF16)

    s1 = _dot_nt(slots(q1), slots(k1))
    s2 = _dot_nt(slots(q2), slots(k2))
    scores = (s1 + jnp.where(same_mid, s2 + jnp.where(same_sub, s3, 0.0), 0.0)).astype(BF16)
    o = _dot_nt(rows(q0).astype(BF16), st.astype(BF16)) + jnp.dot(scores, vb, preferred_element_type=F32)

    w, vd = k3, v3
    band = jnp.sum(q3 * w, axis=-1, keepdims=True) * vd
    for d in range(1, hb):
        w = pltpu.roll(w, 1, 1) * f3
        vd = pltpu.roll(vd, 1, 1)
        sd = jnp.sum(q3 * w, axis=-1, keepdims=True)
        band = band + jnp.where(sub_h >= d, sd, 0.0) * vd
    o = o + band.reshape(C, HEAD_DIM)

    st_new = st * g_all + _dot_tn(vb, rows(kend).astype(BF16))

    ms = jnp.mean(o * o, axis=-1, keepdims=True)
    y = o * lax.rsqrt(ms + EPS) * ng * gate
    return y, st_new


def _hgrn_kernel(q_ref, lf_ref, i_ref, g_ref, ng_ref, o_ref, st_ref):
    C = HG_CHUNK
    n_heads = st_ref.shape[0]
    n_chunks = q_ref.shape[0] // C

    @pl.when(pl.program_id(1) == 0)
    def _():
        st_ref[...] = jnp.zeros_like(st_ref)

    r_i = lax.broadcasted_iota(jnp.int32, (C, C), 0)
    c_i = lax.broadcasted_iota(jnp.int32, (C, C), 1)
    same_mid = (r_i // HG_MID) == (c_i // HG_MID)
    same_sub = (r_i // HG_SUB) == (c_i // HG_SUB)
    sub = lax.broadcasted_iota(jnp.int32, (C // HG_SUB, HG_SUB, 1), 1)
    ng = ng_ref[...]

    def chunk(ci, carry):
        rows = pl.ds(pl.multiple_of(ci * C, C), C)
        for h in range(n_heads):
            sl = pl.ds(h * HEAD_DIM, HEAD_DIM)
            y, st_new = _hgrn_head(
                q_ref[rows, sl].astype(F32), lf_ref[rows, sl].astype(F32), i_ref[rows, sl].astype(F32),
                g_ref[rows, sl].astype(F32), ng, st_ref[h], sub, same_mid, same_sub)
            st_ref[h] = st_new
            o_ref[rows, sl] = y.astype(o_ref.dtype)
        return carry

    lax.fori_loop(0, n_chunks, chunk, 0)


def _hgrn(p, norm_g, *, batch, seq, key_width, val_width):
    assert key_width == val_width
    N = p.shape[0]
    tt = min(HG_ROWS, seq)
    assert tt % HG_CHUNK == 0
    n_heads = key_width // HEAD_DIM
    per_b = seq // tt
    col = lambda g: (lambda bi, t: (bi * per_b + t, g))
    return pl.pallas_call(
        _hgrn_kernel,
        grid=(batch, per_b),
        in_specs=[
            pl.BlockSpec((tt, key_width), col(0)),
            pl.BlockSpec((tt, key_width), col(1)),
            pl.BlockSpec((tt, key_width), col(2)),
            pl.BlockSpec((tt, key_width), col(3)),
            pl.BlockSpec((1, HEAD_DIM), lambda bi, t: (0, 0)),
        ],
        out_specs=pl.BlockSpec((tt, val_width), lambda bi, t: (bi * per_b + t, 0)),
        out_shape=jax.ShapeDtypeStruct((N, val_width), BF16),
        scratch_shapes=[pltpu.VMEM((n_heads, HEAD_DIM, HEAD_DIM), F32)],
        compiler_params=_cparams(("parallel", "arbitrary")),
        name="hgrn2_mixer",
    )(p, p, p, p, norm_g.reshape(1, HEAD_DIM))


def _sgu_kernel(u_ref, v_ref, w_ref, bs_ref, o_ref):
    C = SGU_CHUNK
    n_groups = w_ref.shape[0]
    r_i = lax.broadcasted_iota(jnp.int32, (C, C), 0)
    c_i = lax.broadcasted_iota(jnp.int32, (C, C), 1)
    tril = jnp.where(r_i >= c_i, 1.0, 0.0)
    ws = [(w_ref[g] * tril).astype(BF16) for g in range(n_groups)]
    for ci in range(u_ref.shape[0] // C):
        rows = pl.ds(ci * C, C)
        for g in range(n_groups):
            cols = pl.ds(g * GROUP_DIM, GROUP_DIM)
            s = jnp.dot(ws[g], v_ref[rows, cols], preferred_element_type=F32) + bs_ref[:, g:g + 1]
            o_ref[rows, cols] = (u_ref[rows, cols].astype(F32) * s).astype(o_ref.dtype)


def _sgu(p, w_s, b_s, *, seq, col_u, width):
    N = p.shape[0]
    G, C, _ = w_s.shape
    assert C == SGU_CHUNK and G * GROUP_DIM == width and col_u % width == 0
    tt = min(1024, seq)
    cu = col_u // width
    return pl.pallas_call(
        _sgu_kernel,
        grid=(N // tt,),
        in_specs=[
            pl.BlockSpec((tt, width), lambda i: (i, cu)),
            pl.BlockSpec((tt, width), lambda i: (i, cu + 1)),
            pl.BlockSpec((G, C, C), lambda i: (0, 0, 0)),
            pl.BlockSpec((C, G), lambda i: (0, 0)),
        ],
        out_specs=pl.BlockSpec((tt, width), lambda i: (i, 0)),
        out_shape=jax.ShapeDtypeStruct((N, width), BF16),
        compiler_params=_cparams(("parallel",)),
        name="gmlp_spatial_gating",
    )(p, p, w_s, b_s.T)


def _merge_kernel(oa_ref, ob_ref, ga_ref, gb_ref, wa_ref, wb_ref, wo_ref, x_ref, gt_ref, x1_ref):
    D = x_ref.shape[1]
    acc = None
    for c in range(D // MERGE_TK):
        cs = pl.ds(c * MERGE_TK, MERGE_TK)
        ya = jnp.dot(oa_ref[...], wa_ref[:, cs], preferred_element_type=F32)
        yb = jnp.dot(ob_ref[...], wb_ref[:, cs], preferred_element_type=F32)
        m = jax.nn.sigmoid(ga_ref[:, cs].astype(F32)) * ya + jax.nn.sigmoid(gb_ref[:, cs].astype(F32)) * yb
        part = jnp.dot(m.astype(BF16), wo_ref[cs, :], preferred_element_type=F32)
        acc = part if acc is None else acc + part
    x1_ref[...] = x_ref[...] + gt_ref[...] * acc


def _merge(x, p, oa, ob, wa, wb, wo, l, mod, *, seq, col_ga):
    N, D = x.shape
    Wa = oa.shape[1]
    tm = min(256, seq)
    assert col_ga % D == 0 and D % MERGE_TK == 0
    per_b = seq // tm
    cga = col_ga // D
    resident = pl.Buffered(1)
    return pl.pallas_call(
        _merge_kernel,
        grid=(N // tm,),
        in_specs=[
            pl.BlockSpec((tm, Wa), lambda i: (i, 0)),
            pl.BlockSpec((tm, Wa), lambda i: (i, 0)),
            pl.BlockSpec((tm, D), lambda i: (i, cga)),
            pl.BlockSpec((tm, D), lambda i: (i, cga + 1)),
            pl.BlockSpec((None, Wa, D), lambda i: (l, 0, 0), pipeline_mode=resident),
            pl.BlockSpec((None, Wa, D), lambda i: (l, 0, 0), pipeline_mode=resident),
            pl.BlockSpec((None, D, D), lambda i: (l, 0, 0), pipeline_mode=resident),
            pl.BlockSpec((tm, D), lambda i: (i, 0)),
            pl.BlockSpec((None, 1, D), lambda i: ((i // per_b) * N_MOD + 2, 0, 0)),
        ],
        out_specs=pl.BlockSpec((tm, D), lambda i: (i, 0)),
        out_shape=jax.ShapeDtypeStruct((N, D), F32),
        compiler_params=_cparams(("parallel",)),
        name="merge_out_projection",
    )(oa, ob, p, p, wa, wb, wo, x, mod)


def _ffn_kernel(x_ref, g_ref, sh_ref, sc_ref, gt_ref, w1_ref, w2_ref, fg_ref, o_ref, h_ref, *, final_norm):
    kk = pl.program_id(1)

    def gated_chunk(h):
        hid = jnp.dot(h, w1_ref[...], preferred_element_type=F32)
        act = jnp.square(jnp.maximum(hid, 0.0)).astype(BF16)
        return gt_ref[...] * jnp.dot(act, w2_ref[...], preferred_element_type=F32)

    @pl.when(kk == 0)
    def _():
        x = x_ref[...]
        h = _modnorm(x, g_ref[...], sh_ref[...], sc_ref[...]).astype(BF16)
        h_ref[...] = h
        o_ref[...] = x + gated_chunk(h)

    @pl.when(kk > 0)
    def _():
        o_ref[...] += gated_chunk(h_ref[...])

    if final_norm:
        @pl.when(kk == pl.num_programs(1) - 1)
        def _():
            x2 = o_ref[...]
            ms = jnp.mean(x2 * x2, axis=-1, keepdims=True)
            o_ref[...] = x2 * lax.rsqrt(ms + EPS) * fg_ref[...]


def _ffn(x1, w1, w2, l, mod, norm2_g, final_g, *, seq, final_norm):
    N, D = x1.shape
    th = w1.shape[3]
    FF = w1.shape[1] * th
    tm = min(1024, seq)
    per_b = seq // tm
    modspec = lambda which: pl.BlockSpec((None, 1, D), lambda i, k: ((i // per_b) * N_MOD + which, 0, 0))
    return pl.pallas_call(
        functools.partial(_ffn_kernel, final_norm=final_norm),
        grid=(N // tm, FF // th),
        in_specs=[
            pl.BlockSpec((tm, D), lambda i, k: (i, 0)),
            pl.BlockSpec((1, D), lambda i, k: (0, 0)),
            modspec(3),
            modspec(4),
            modspec(5),
            pl.BlockSpec((None, None, D, th), lambda i, k: (l, k, 0, 0)),
            pl.BlockSpec((None, th, D), lambda i, k: (l, k, 0)),
            pl.BlockSpec((1, D), lambda i, k: (0, 0)),
        ],
        out_specs=pl.BlockSpec((tm, D), lambda i, k: (i, 0)),
        out_shape=jax.ShapeDtypeStruct((N, D), F32),
        scratch_shapes=[pltpu.VMEM((tm, D), BF16)],
        compiler_params=_cparams(("parallel", "arbitrary")),
        name="relu2_mlp",
    )(x1, norm2_g.reshape(1, D), mod, mod, mod, w1, w2, final_g.reshape(1, D))


def kernel(x, c, w_ada, b_ada, norm1_g, norm2_g, w_in, hgrn_lower_bounds, hgrn_norm_g, ln_v_g, ln_v_b,
           w_spatial, b_spatial, w_proj_a, w_proj_b, w_out, w_ff_in, w_ff_out, final_norm_g):
    B, T, D = x.shape
    L = w_ada.shape[0]
    N = B * T
    key_width = hgrn_lower_bounds.shape[1]
    val_width = w_proj_a.shape[1]
    sgu_width = w_proj_b.shape[1]
    col_u = 2 * key_width + 2 * val_width
    col_ga = col_u + 2 * sgu_width

    mod_all = _modulation(c, w_ada, b_ada)
    lbs = _lower_bounds(hgrn_lower_bounds)
    w_in_b = _chunk_cols(w_in, INPROJ_TN)
    w1_b = _chunk_cols(w_ff_in, FFN_TH)
    wa_b, wb_b = w_proj_a.astype(BF16), w_proj_b.astype(BF16)
    wo_b, w2_b = w_out.astype(BF16), w_ff_out.astype(BF16)

    xf = x.reshape(N, D)
    for l in range(L):
        mod = mod_all[l].reshape(B * N_MOD, 1, D)
        p = _inproj(xf, mod, norm1_g[l], w_in_b, l, lbs[l], ln_v_g[l], ln_v_b[l], seq=T)
        oa = _hgrn(p, hgrn_norm_g[l], batch=B, seq=T, key_width=key_width, val_width=val_width)
        ob = _sgu(p, w_spatial[l], b_spatial[l], seq=T, col_u=col_u, width=sgu_width)
        x1 = _merge(xf, p, oa, ob, wa_b, wb_b, wo_b, l, mod, seq=T, col_ga=col_ga)
        xf = _ffn(x1, w1_b, w2_b, l, mod, norm2_g[l], final_norm_g, seq=T, final_norm=(l == L - 1))
    return xf.reshape(B, T, D)
```

```python
import functools

import jax
import jax.numpy as jnp
from jax import lax
from jax.experimental import pallas as pl
from jax.experimental.pallas import tpu as pltpu

F32 = jnp.float32
BF16 = jnp.bfloat16

EPS = 1e-6
MIN_FORGET = 1e-20
N_MOD = 6
HEAD_DIM = 128
GROUP_DIM = 128
SGU_CHUNK = 128
HG_CHUNK = 128
HG_MID = 32
HG_SUB = 8
MIX_ROWS = 256
VMEM_LIMIT = 60 * 1024 * 1024
INPROJ_TN = 2048
MERGE_TK = 1024
FFN_TH = 1024


def _chunk_cols(w, t):
    L, R, C = w.shape
    t = min(t, C)
    return w.reshape(L, R, C // t, t).transpose(0, 2, 1, 3).astype(BF16)


def _cparams(sem):
    return pltpu.CompilerParams(dimension_semantics=sem, vmem_limit_bytes=VMEM_LIMIT)


def _silu(x):
    return x * jax.nn.sigmoid(x)


def _gelu(x):
    return 0.5 * x * (1.0 + lax.erf(x * (0.5 ** 0.5)))


def _modnorm(x, g, shift, scale):
    ms = jnp.mean(x * x, axis=-1, keepdims=True)
    y = x * lax.rsqrt(ms + EPS) * g
    return y * (1.0 + scale) + shift


def _mod_kernel(c_ref, w_ref, b_ref, o_ref):
    c = c_ref[...]
    ca = _silu(c).astype(BF16)
    o_ref[...] = jnp.dot(ca, w_ref[...].astype(BF16), preferred_element_type=F32) + b_ref[...]


def _modulation(c, w_ada, b_ada):
    L, D, W = w_ada.shape
    B = c.shape[0]
    tn = min(2048, W)
    return pl.pallas_call(
        _mod_kernel,
        grid=(L, W // tn),
        in_specs=[
            pl.BlockSpec((B, D), lambda l, j: (0, 0)),
            pl.BlockSpec((None, D, tn), lambda l, j: (l, 0, j)),
            pl.BlockSpec((None, 1, tn), lambda l, j: (l, 0, j)),
        ],
        out_specs=pl.BlockSpec((None, B, tn), lambda l, j: (l, 0, j)),
        out_shape=jax.ShapeDtypeStruct((L, B, W), F32),
        compiler_params=_cparams(("arbitrary", "arbitrary")),
        name="adaln_modulation",
    )(c, w_ada, b_ada.reshape(L, 1, W))


def _lower_bounds_kernel(lb_ref, o_ref):
    x = lb_ref[...]
    m = jnp.max(x, axis=0, keepdims=True)
    e = jnp.exp(x - m)
    p = e / jnp.sum(e, axis=0, keepdims=True)
    L = x.shape[0]
    acc = jnp.zeros_like(p[0:1])
    rows = []
    for l in range(L):
        acc = acc + p[l:l + 1]
        rows.append(acc - p[0:1])
    o_ref[...] = jnp.concatenate(rows, axis=0)


def _lower_bounds(lb):
    return pl.pallas_call(
        _lower_bounds_kernel,
        out_shape=jax.ShapeDtypeStruct(lb.shape, F32),
        name="hgrn_lower_bounds",
    )(lb)


def _inproj_kernel(x_ref, sh_ref, sc_ref, g_ref, w_ref, lb_ref, lng_ref, lnb_ref, p_ref, h_ref):
    j = pl.program_id(1)
    half = w_ref.shape[1] // 2

    def project():
        return jnp.dot(h_ref[...], w_ref[...], preferred_element_type=F32)

    @pl.when(j == 0)
    def _():
        h_ref[...] = _modnorm(x_ref[...], g_ref[...], sh_ref[...], sc_ref[...]).astype(BF16)
        acc = project()
        lb = lb_ref[...]
        f = lb + (1.0 - lb) * jax.nn.sigmoid(acc[:, half:])
        p_ref[:, :half] = _silu(acc[:, :half]).astype(p_ref.dtype)
        p_ref[:, half:] = jnp.log(jnp.maximum(f, MIN_FORGET)).astype(p_ref.dtype)

    @pl.when(j == 1)
    def _():
        acc = project()
        p_ref[:, :half] = acc[:, :half].astype(p_ref.dtype)
        p_ref[:, half:] = _silu(acc[:, half:]).astype(p_ref.dtype)

    @pl.when(j == 2)
    def _():
        acc = project()
        p_ref[:, :half] = _gelu(acc[:, :half]).astype(p_ref.dtype)
        vg = _gelu(acc[:, half:])
        mu = jnp.mean(vg, axis=-1, keepdims=True)
        vc = vg - mu
        var = jnp.mean(vc * vc, axis=-1, keepdims=True)
        p_ref[:, half:] = (vc * lax.rsqrt(var + EPS) * lng_ref[...] + lnb_ref[...]).astype(p_ref.dtype)

    @pl.when(j >= 3)
    def _():
        p_ref[...] = project().astype(p_ref.dtype)


def _inproj(x, mod, norm_g, w, l, lb, ln_g, ln_b, *, seq):
    N, D = x.shape
    tn = w.shape[3]
    W = w.shape[1] * tn
    half = tn // 2
    assert lb.shape[-1] == half and ln_g.shape[-1] == half and W == 3 * tn + 2 * D and D == tn
    tm = min(1024, seq)
    per_b = seq // tm
    return pl.pallas_call(
        _inproj_kernel,
        grid=(N // tm, W // tn),
        in_specs=[
            pl.BlockSpec((tm, D), lambda i, j: (i, 0)),
            pl.BlockSpec((None, 1, D), lambda i, j: ((i // per_b) * N_MOD + 0, 0, 0)),
            pl.BlockSpec((None, 1, D), lambda i, j: ((i // per_b) * N_MOD + 1, 0, 0)),
            pl.BlockSpec((1, D), lambda i, j: (0, 0)),
            pl.BlockSpec((None, None, D, tn), lambda i, j: (l, j, 0, 0)),
            pl.BlockSpec((1, half), lambda i, j: (0, 0)),
            pl.BlockSpec((1, half), lambda i, j: (0, 0)),
            pl.BlockSpec((1, half), lambda i, j: (0, 0)),
        ],
        out_specs=pl.BlockSpec((tm, tn), lambda i, j: (i, j)),
        out_shape=jax.ShapeDtypeStruct((N, W), BF16),
        scratch_shapes=[pltpu.VMEM((tm, D), BF16)],
        compiler_params=_cparams(("parallel", "arbitrary")),
        name="in_projection",
    )(x, mod, mod, norm_g.reshape(1, D), w, lb.reshape(1, half), ln_g.reshape(1, half), ln_b.reshape(1, half))


def _dot_nt(a, b):
    return lax.dot_general(a, b, (((1,), (1,)), ((), ())), preferred_element_type=F32)


def _dot_tn(a, b):
    return lax.dot_general(a, b, (((0,), (0,)), ((), ())), preferred_element_type=F32)


def _hgrn_head(qs, lf, vr, gate, ng, st, sub, same_mid, same_sub):
    C = HG_CHUNK
    nb = C // HG_SUB
    n_mid = C // HG_MID
    n_sub = HG_MID // HG_SUB
    hb = HG_SUB // 2
    shp3 = (nb, HG_SUB, HEAD_DIM)

    q3 = qs.reshape(shp3)
    v3 = vr.reshape(shp3)
    vb = vr.astype(BF16)
    hi = sub >= hb
    sub_h = jnp.where(hi, sub - hb, sub)

    a = lf.reshape(shp3)
    f3 = jnp.exp(a)
    k3 = 1.0 - f3
    for sh in (1, 2):
        a = a + jnp.where(sub_h >= sh, pltpu.roll(a, sh, 1), 0.0)
    e4 = jnp.exp(a)
    tot = jnp.where(hi, a[:, HG_SUB - 1:HG_SUB, :], a[:, hb - 1:hb, :])
    qe4 = q3 * e4
    ke4 = k3 * jnp.exp(tot - a)
    g_lo, g_hi = e4[:, hb - 1:hb, :], e4[:, HG_SUB - 1:HG_SUB, :]
    s3 = _dot_nt(jnp.where(hi, qe4, 0.0).reshape(C, HEAD_DIM).astype(BF16),
                 jnp.where(hi, 0.0, ke4).reshape(C, HEAD_DIM).astype(BF16))
    qe = jnp.where(hi, qe4 * g_lo, qe4)
    ke = jnp.where(hi, ke4, ke4 * g_hi)
    g8 = g_lo * g_hi
    g = [g8[j] for j in range(nb)]

    def bc(r):
        return jnp.broadcast_to(r, (HG_SUB, HEAD_DIM))

    pre, suf, g_mid = [], [], []
    for mi in range(n_mid):
        gs = g[mi * n_sub:(mi + 1) * n_sub]
        p = [None]
        for i in range(1, n_sub):
            p.append(gs[i - 1] if p[-1] is None else p[-1] * gs[i - 1])
        s = [None] * n_sub
        for i in range(n_sub - 2, -1, -1):
            s[i] = gs[i + 1] if s[i + 1] is None else s[i + 1] * gs[i + 1]
        pre.append(p)
        suf.append(s)
        g_mid.append(p[-1] * gs[-1])

    zero = jnp.zeros((HG_SUB, HEAD_DIM), F32)
    q2 = [[zero] * nb for _ in range(n_sub - 1)]
    k2 = [[zero] * nb for _ in range(n_sub - 1)]
    q1 = [[zero] * nb for _ in range(n_mid - 1)]
    k1 = [[zero] * nb for _ in range(n_mid - 1)]
    q0, kend = [None] * nb, [None] * nb
    for mi in range(n_mid):
        gq = [None] * (mi + 1)
        for jm in range(mi - 1, -1, -1):
            gq[jm] = g_mid[jm] if gq[jm + 1] is None else gq[jm + 1] * g_mid[jm]
        gqb = [None if r is None else bc(r) for r in gq]
        gk = None
        for m2 in range(mi + 1, n_mid):
            gk = g_mid[m2] if gk is None else gk * g_mid[m2]
        gkb = None if gk is None else bc(gk)
        for i in range(n_sub):
            j = mi * n_sub + i
            if i < n_sub - 1:
                k2[i][j] = ke[j]
            m = qe[j]
            for jj in range(i, 0, -1):
                q2[jj - 1][j] = m
                if jj > 1:
                    m = m * bc(g[mi * n_sub + jj - 1])
            qm = qe[j] if pre[mi][i] is None else qe[j] * bc(pre[mi][i])
            km = ke[j] if suf[mi][i] is None else ke[j] * bc(suf[mi][i])
            if mi < n_mid - 1:
                k1[mi][j] = km
            for jm in range(1, mi + 1):
                q1[jm - 1][j] = qm if gqb[jm] is None else qm * gqb[jm]
            q0[j] = qm if gqb[0] is None else qm * gqb[0]
            kend[j] = km if gkb is None else km * gkb
    g_all = g_mid[0]
    for mi in range(1, n_mid):
        g_all = g_all * g_mid[mi]

    def rows(pieces):
        return jnp.concatenate(pieces, axis=0)

    def slots(groups):
        return jnp.concatenate([rows(p) for p in groups], axis=1).astype(BF16)

    s1 = _dot_nt(slots(q1), slots(k1))
    s2 = _dot_nt(slots(q2), slots(k2))
    scores = (s1 + jnp.where(same_mid, s2 + jnp.where(same_sub, s3, 0.0), 0.0)).astype(BF16)
    o = _dot_nt(rows(q0).astype(BF16), st.astype(BF16)) + jnp.dot(scores, vb, preferred_element_type=F32)

    w, vd = k3, v3
    band = jnp.sum(q3 * w, axis=-1, keepdims=True) * vd
    for d in range(1, hb):
        w = pltpu.roll(w, 1, 1) * f3
        vd = pltpu.roll(vd, 1, 1)
        sd = jnp.sum(q3 * w, axis=-1, keepdims=True)
        band = band + jnp.where(sub_h >= d, sd, 0.0) * vd
    o = o + band.reshape(C, HEAD_DIM)

    st_new = st * g_all + _dot_tn(vb, rows(kend).astype(BF16))

    ms = jnp.mean(o * o, axis=-1, keepdims=True)
    y = o * lax.rsqrt(ms + EPS) * ng * gate
    return y, st_new


def _mix_merge_kernel(q_ref, lf_ref, i_ref, g_ref, u_ref, v_ref, ng_ref, ws_ref, bs_ref,
                      ga_ref, gb_ref, wa_ref, wb_ref, wo_ref, x_ref, gt_ref,
                      x1_ref, st_ref, oa_ref, ob_ref, *, tiles_per_seq):
    C = HG_CHUNK
    n_heads = st_ref.shape[0]
    n_groups = ws_ref.shape[0]
    tm, D = x_ref.shape

    @pl.when(lax.rem(pl.program_id(0), tiles_per_seq) == 0)
    def _():
        st_ref[...] = jnp.zeros_like(st_ref)

    r_i = lax.broadcasted_iota(jnp.int32, (C, C), 0)
    c_i = lax.broadcasted_iota(jnp.int32, (C, C), 1)
    same_mid = (r_i // HG_MID) == (c_i // HG_MID)
    same_sub = (r_i // HG_SUB) == (c_i // HG_SUB)
    sub = lax.broadcasted_iota(jnp.int32, (C // HG_SUB, HG_SUB, 1), 1)
    tril = jnp.where(r_i >= c_i, 1.0, 0.0)
    ng = ng_ref[...]
    ws = [(ws_ref[g] * tril).astype(BF16) for g in range(n_groups)]
    for ci in range(tm // C):
        rows = pl.ds(ci * C, C)
        for h in range(n_heads):
            sl = pl.ds(h * HEAD_DIM, HEAD_DIM)
            y, st_new = _hgrn_head(
                q_ref[rows, sl].astype(F32), lf_ref[rows, sl].astype(F32), i_ref[rows, sl].astype(F32),
                g_ref[rows, sl].astype(F32), ng, st_ref[h], sub, same_mid, same_sub)
            st_ref[h] = st_new
            oa_ref[rows, sl] = y.astype(BF16)
        for g in range(n_groups):
            cols = pl.ds(g * GROUP_DIM, GROUP_DIM)
            sg = jnp.dot(ws[g], v_ref[rows, cols], preferred_element_type=F32) + bs_ref[:, g:g + 1]
            ob_ref[rows, cols] = (u_ref[rows, cols].astype(F32) * sg).astype(BF16)

    acc = None
    for c in range(D // MERGE_TK):
        cs = pl.ds(c * MERGE_TK, MERGE_TK)
        ya = jnp.dot(oa_ref[...], wa_ref[:, cs], preferred_element_type=F32)
        yb = jnp.dot(ob_ref[...], wb_ref[:, cs], preferred_element_type=F32)
        m = jax.nn.sigmoid(ga_ref[:, cs].astype(F32)) * ya + jax.nn.sigmoid(gb_ref[:, cs].astype(F32)) * yb
        part = jnp.dot(m.astype(BF16), wo_ref[cs, :], preferred_element_type=F32)
        acc = part if acc is None else acc + part
    x1_ref[...] = x_ref[...] + gt_ref[...] * acc


def _mix_merge(x, p, norm_g, w_s, b_s, wa, wb, wo, l, mod, *, seq, key_width, sgu_width):
    N, D = x.shape
    G, C, _ = w_s.shape
    assert key_width == sgu_width and C == SGU_CHUNK and G * GROUP_DIM == sgu_width and D == 2 * key_width
    kw = key_width
    tm = min(MIX_ROWS, seq)
    assert tm % HG_CHUNK == 0 and seq % tm == 0
    per_b = seq // tm
    n_heads = kw // HEAD_DIM
    col = lambda c: (lambda s: (s, c))
    resident = pl.Buffered(1)
    return pl.pallas_call(
        functools.partial(_mix_merge_kernel, tiles_per_seq=per_b),
        grid=(N // tm,),
        in_specs=[
            pl.BlockSpec((tm, kw), col(0)),
            pl.BlockSpec((tm, kw), col(1)),
            pl.BlockSpec((tm, kw), col(2)),
            pl.BlockSpec((tm, kw), col(3)),
            pl.BlockSpec((tm, kw), col(4)),
            pl.BlockSpec((tm, kw), col(5)),
            pl.BlockSpec((1, HEAD_DIM), lambda s: (0, 0)),
            pl.BlockSpec((G, C, C), lambda s: (0, 0, 0)),
            pl.BlockSpec((C, G), lambda s: (0, 0)),
            pl.BlockSpec((tm, D), col(3)),
            pl.BlockSpec((tm, D), col(4)),
            pl.BlockSpec((None, kw, D), lambda s: (l, 0, 0), pipeline_mode=resident),
            pl.BlockSpec((None, kw, D), lambda s: (l, 0, 0), pipeline_mode=resident),
            pl.BlockSpec((None, D, D), lambda s: (l, 0, 0), pipeline_mode=resident),
            pl.BlockSpec((tm, D), col(0)),
            pl.BlockSpec((None, 1, D), lambda s: ((s // per_b) * N_MOD + 2, 0, 0)),
        ],
        out_specs=pl.BlockSpec((tm, D), col(0)),
        out_shape=jax.ShapeDtypeStruct((N, D), F32),
        scratch_shapes=[
            pltpu.VMEM((n_heads, HEAD_DIM, HEAD_DIM), F32),
            pltpu.VMEM((tm, kw), BF16),
            pltpu.VMEM((tm, kw), BF16),
        ],
        compiler_params=_cparams(("arbitrary",)),
        name="mixers_merge",
    )(p, p, p, p, p, p, norm_g.reshape(1, HEAD_DIM), w_s, b_s.T, p, p, wa, wb, wo, x, mod)


def _ffn_kernel(x_ref, g_ref, sh_ref, sc_ref, gt_ref, w1_ref, w2_ref, fg_ref, o_ref, h_ref, *, final_norm):
    kk = pl.program_id(1)

    def gated_chunk(h):
        hid = jnp.dot(h, w1_ref[...], preferred_element_type=F32)
        act = jnp.square(jnp.maximum(hid, 0.0)).astype(BF16)
        return gt_ref[...] * jnp.dot(act, w2_ref[...], preferred_element_type=F32)

    @pl.when(kk == 0)
    def _():
        x = x_ref[...]
        h = _modnorm(x, g_ref[...], sh_ref[...], sc_ref[...]).astype(BF16)
        h_ref[...] = h
        o_ref[...] = x + gated_chunk(h)

    @pl.when(kk > 0)
    def _():
        o_ref[...] += gated_chunk(h_ref[...])

    if final_norm:
        @pl.when(kk == pl.num_programs(1) - 1)
        def _():
            x2 = o_ref[...]
            ms = jnp.mean(x2 * x2, axis=-1, keepdims=True)
            o_ref[...] = x2 * lax.rsqrt(ms + EPS) * fg_ref[...]


def _ffn(x1, w1, w2, l, mod, norm2_g, final_g, *, seq, final_norm):
    N, D = x1.shape
    th = w1.shape[3]
    FF = w1.shape[1] * th
    tm = min(1024, seq)
    per_b = seq // tm
    modspec = lambda which: pl.BlockSpec((None, 1, D), lambda i, k: ((i // per_b) * N_MOD + which, 0, 0))
    return pl.pallas_call(
        functools.partial(_ffn_kernel, final_norm=final_norm),
        grid=(N // tm, FF // th),
        in_specs=[
            pl.BlockSpec((tm, D), lambda i, k: (i, 0)),
            pl.BlockSpec((1, D), lambda i, k: (0, 0)),
            modspec(3),
            modspec(4),
            modspec(5),
            pl.BlockSpec((None, None, D, th), lambda i, k: (l, k, 0, 0)),
            pl.BlockSpec((None, th, D), lambda i, k: (l, k, 0)),
            pl.BlockSpec((1, D), lambda i, k: (0, 0)),
        ],
        out_specs=pl.BlockSpec((tm, D), lambda i, k: (i, 0)),
        out_shape=jax.ShapeDtypeStruct((N, D), F32),
        scratch_shapes=[pltpu.VMEM((tm, D), BF16)],
        compiler_params=_cparams(("parallel", "arbitrary")),
        name="relu2_mlp",
    )(x1, norm2_g.reshape(1, D), mod, mod, mod, w1, w2, final_g.reshape(1, D))


def kernel(x, c, w_ada, b_ada, norm1_g, norm2_g, w_in, hgrn_lower_bounds, hgrn_norm_g, ln_v_g, ln_v_b,
           w_spatial, b_spatial, w_proj_a, w_proj_b, w_out, w_ff_in, w_ff_out, final_norm_g):
    B, T, D = x.shape
    L = w_ada.shape[0]
    N = B * T
    key_width = hgrn_lower_bounds.shape[1]
    sgu_width = w_proj_b.shape[1]
    assert w_proj_a.shape[1] == key_width

    mod_all = _modulation(c, w_ada, b_ada)
    lbs = _lower_bounds(hgrn_lower_bounds)
    w_in_b = _chunk_cols(w_in, INPROJ_TN)
    w1_b = _chunk_cols(w_ff_in, FFN_TH)
    wa_b, wb_b = w_proj_a.astype(BF16), w_proj_b.astype(BF16)
    wo_b, w2_b = w_out.astype(BF16), w_ff_out.astype(BF16)

    xf = x.reshape(N, D)
    for l in range(L):
        mod = mod_all[l].reshape(B * N_MOD, 1, D)
        p = _inproj(xf, mod, norm1_g[l], w_in_b, l, lbs[l], ln_v_g[l], ln_v_b[l], seq=T)
        x1 = _mix_merge(xf, p, hgrn_norm_g[l], w_spatial[l], b_spatial[l], wa_b, wb_b, wo_b, l, mod,
                        seq=T, key_width=key_width, sgu_width=sgu_width)
        xf = _ffn(x1, w1_b, w2_b, l, mod, norm2_g[l], final_norm_g, seq=T, final_norm=(l == L - 1))
    return xf.reshape(B, T, D)
```

```python
import functools

import jax
import jax.numpy as jnp
from jax import lax
from jax.experimental import pallas as pl
from jax.experimental.pallas import tpu as pltpu

F32 = jnp.float32
BF16 = jnp.bfloat16

EPS = 1e-6
MIN_FORGET = 1e-20
N_MOD = 6
HEAD_DIM = 128
GROUP_DIM = 128
SGU_CHUNK = 128
HG_CHUNK = 128
HG_MID = 32
HG_SUB = 8
MIX_ROWS = 256
VMEM_LIMIT = 60 * 1024 * 1024
INPROJ_TN = 2048
MERGE_TK = 1024
FFN_TH = 1024


def _cparams(sem):
    return pltpu.CompilerParams(dimension_semantics=sem, vmem_limit_bytes=VMEM_LIMIT)


def _silu(x):
    return x * jax.nn.sigmoid(x)


def _gelu(x):
    return 0.5 * x * (1.0 + lax.erf(x * (0.5 ** 0.5)))


def _modnorm(x, g, shift, scale):
    ms = jnp.mean(x * x, axis=-1, keepdims=True)
    y = x * lax.rsqrt(ms + EPS) * g
    return y * (1.0 + scale) + shift


def _mod_kernel(c_ref, w_ref, b_ref, o_ref):
    c = c_ref[...]
    ca = _silu(c).astype(BF16)
    o_ref[...] = jnp.dot(ca, w_ref[...].astype(BF16), preferred_element_type=F32) + b_ref[...]


def _modulation(c, w_ada, b_ada):
    L, D, W = w_ada.shape
    B = c.shape[0]
    tn = min(2048, W)
    return pl.pallas_call(
        _mod_kernel,
        grid=(L, W // tn),
        in_specs=[
            pl.BlockSpec((B, D), lambda l, j: (0, 0)),
            pl.BlockSpec((None, D, tn), lambda l, j: (l, 0, j)),
            pl.BlockSpec((None, 1, tn), lambda l, j: (l, 0, j)),
        ],
        out_specs=pl.BlockSpec((None, B, tn), lambda l, j: (l, 0, j)),
        out_shape=jax.ShapeDtypeStruct((L, B, W), F32),
        compiler_params=_cparams(("arbitrary", "arbitrary")),
        name="adaln_modulation",
    )(c, w_ada, b_ada.reshape(L, 1, W))


def _lower_bounds_kernel(lb_ref, o_ref):
    x = lb_ref[...]
    m = jnp.max(x, axis=0, keepdims=True)
    e = jnp.exp(x - m)
    p = e / jnp.sum(e, axis=0, keepdims=True)
    L = x.shape[0]
    acc = jnp.zeros_like(p[0:1])
    rows = []
    for l in range(L):
        acc = acc + p[l:l + 1]
        rows.append(acc - p[0:1])
    o_ref[...] = jnp.concatenate(rows, axis=0)


def _lower_bounds(lb):
    return pl.pallas_call(
        _lower_bounds_kernel,
        out_shape=jax.ShapeDtypeStruct(lb.shape, F32),
        name="hgrn_lower_bounds",
    )(lb)


def _inproj_kernel(x_ref, sh_ref, sc_ref, g_ref, w_ref, lb_ref, lng_ref, lnb_ref, p_ref, h_ref):
    j = pl.program_id(1)
    half = w_ref.shape[1] // 2

    def project():
        return jnp.dot(h_ref[...], w_ref[...], preferred_element_type=F32)

    @pl.when(j == 0)
    def _():
        h_ref[...] = _modnorm(x_ref[...], g_ref[...], sh_ref[...], sc_ref[...]).astype(BF16)
        acc = project()
        lb = lb_ref[...]
        f = lb + (1.0 - lb) * jax.nn.sigmoid(acc[:, half:])
        p_ref[:, :half] = _silu(acc[:, :half]).astype(p_ref.dtype)
        p_ref[:, half:] = jnp.log(jnp.maximum(f, MIN_FORGET)).astype(p_ref.dtype)

    @pl.when(j == 1)
    def _():
        acc = project()
        p_ref[:, :half] = acc[:, :half].astype(p_ref.dtype)
        p_ref[:, half:] = _silu(acc[:, half:]).astype(p_ref.dtype)

    @pl.when(j == 2)
    def _():
        acc = project()
        p_ref[:, :half] = _gelu(acc[:, :half]).astype(p_ref.dtype)
        vg = _gelu(acc[:, half:])
        mu = jnp.mean(vg, axis=-1, keepdims=True)
        vc = vg - mu
        var = jnp.mean(vc * vc, axis=-1, keepdims=True)
        p_ref[:, half:] = (vc * lax.rsqrt(var + EPS) * lng_ref[...] + lnb_ref[...]).astype(p_ref.dtype)

    @pl.when(j >= 3)
    def _():
        p_ref[...] = project().astype(p_ref.dtype)


def _inproj(x, mod, norm_g, w, lb, ln_g, ln_b, *, seq):
    N, D = x.shape
    W = w.shape[1]
    tn = INPROJ_TN
    half = tn // 2
    assert lb.shape[-1] == half and ln_g.shape[-1] == half and W == 3 * tn + 2 * D and D == tn
    tm = min(1024, seq)
    per_b = seq // tm
    return pl.pallas_call(
        _inproj_kernel,
        grid=(N // tm, W // tn),
        in_specs=[
            pl.BlockSpec((tm, D), lambda i, j: (i, 0)),
            pl.BlockSpec((None, 1, D), lambda i, j: ((i // per_b) * N_MOD + 0, 0, 0)),
            pl.BlockSpec((None, 1, D), lambda i, j: ((i // per_b) * N_MOD + 1, 0, 0)),
            pl.BlockSpec((1, D), lambda i, j: (0, 0)),
            pl.BlockSpec((D, tn), lambda i, j: (0, j)),
            pl.BlockSpec((1, half), lambda i, j: (0, 0)),
            pl.BlockSpec((1, half), lambda i, j: (0, 0)),
            pl.BlockSpec((1, half), lambda i, j: (0, 0)),
        ],
        out_specs=pl.BlockSpec((tm, tn), lambda i, j: (i, j)),
        out_shape=jax.ShapeDtypeStruct((N, W), BF16),
        scratch_shapes=[pltpu.VMEM((tm, D), BF16)],
        compiler_params=_cparams(("parallel", "arbitrary")),
        name="in_projection",
    )(x, mod, mod, norm_g.reshape(1, D), w, lb.reshape(1, half), ln_g.reshape(1, half), ln_b.reshape(1, half))


def _dot_nt(a, b):
    return lax.dot_general(a, b, (((1,), (1,)), ((), ())), preferred_element_type=F32)


def _dot_tn(a, b):
    return lax.dot_general(a, b, (((0,), (0,)), ((), ())), preferred_element_type=F32)


def _hgrn_head(qs, lf, vr, gate, ng, st, sub, same_mid, same_sub):
    C = HG_CHUNK
    nb = C // HG_SUB
    n_mid = C // HG_MID
    n_sub = HG_MID // HG_SUB
    hb = HG_SUB // 2
    shp3 = (nb, HG_SUB, HEAD_DIM)

    q3 = qs.reshape(shp3)
    v3 = vr.reshape(shp3)
    vb = vr.astype(BF16)
    hi = sub >= hb
    sub_h = jnp.where(hi, sub - hb, sub)

    a = lf.reshape(shp3)
    f3 = jnp.exp(a)
    k3 = 1.0 - f3
    for sh in (1, 2):
        a = a + jnp.where(sub_h >= sh, pltpu.roll(a, sh, 1), 0.0)
    e4 = jnp.exp(a)
    tot = jnp.where(hi, a[:, HG_SUB - 1:HG_SUB, :], a[:, hb - 1:hb, :])
    qe4 = q3 * e4
    ke4 = k3 * jnp.exp(tot - a)
    g_lo, g_hi = e4[:, hb - 1:hb, :], e4[:, HG_SUB - 1:HG_SUB, :]
    s3 = _dot_nt(jnp.where(hi, qe4, 0.0).reshape(C, HEAD_DIM).astype(BF16),
                 jnp.where(hi, 0.0, ke4).reshape(C, HEAD_DIM).astype(BF16))
    qe = jnp.where(hi, qe4 * g_lo, qe4)
    ke = jnp.where(hi, ke4, ke4 * g_hi)
    g8 = g_lo * g_hi
    g = [g8[j] for j in range(nb)]

    def bc(r):
        return jnp.broadcast_to(r, (HG_SUB, HEAD_DIM))

    pre, suf, g_mid = [], [], []
    for mi in range(n_mid):
        gs = g[mi * n_sub:(mi + 1) * n_sub]
        p = [None]
        for i in range(1, n_sub):
            p.append(gs[i - 1] if p[-1] is None else p[-1] * gs[i - 1])
        s = [None] * n_sub
        for i in range(n_sub - 2, -1, -1):
            s[i] = gs[i + 1] if s[i + 1] is None else s[i + 1] * gs[i + 1]
        pre.append(p)
        suf.append(s)
        g_mid.append(p[-1] * gs[-1])

    zero = jnp.zeros((HG_SUB, HEAD_DIM), F32)
    q2 = [[zero] * nb for _ in range(n_sub - 1)]
    k2 = [[zero] * nb for _ in range(n_sub - 1)]
    q1 = [[zero] * nb for _ in range(n_mid - 1)]
    k1 = [[zero] * nb for _ in range(n_mid - 1)]
    q0, kend = [None] * nb, [None] * nb
    for mi in range(n_mid):
        gq = [None] * (mi + 1)
        for jm in range(mi - 1, -1, -1):
            gq[jm] = g_mid[jm] if gq[jm + 1] is None else gq[jm + 1] * g_mid[jm]
        gqb = [None if r is None else bc(r) for r in gq]
        gk = None
        for m2 in range(mi + 1, n_mid):
            gk = g_mid[m2] if gk is None else gk * g_mid[m2]
        gkb = None if gk is None else bc(gk)
        for i in range(n_sub):
            j = mi * n_sub + i
            if i < n_sub - 1:
                k2[i][j] = ke[j]
            m = qe[j]
            for jj in range(i, 0, -1):
                q2[jj - 1][j] = m
                if jj > 1:
                    m = m * bc(g[mi * n_sub + jj - 1])
            qm = qe[j] if pre[mi][i] is None else qe[j] * bc(pre[mi][i])
            km = ke[j] if suf[mi][i] is None else ke[j] * bc(suf[mi][i])
            if mi < n_mid - 1:
                k1[mi][j] = km
            for jm in range(1, mi + 1):
                q1[jm - 1][j] = qm if gqb[jm] is None else qm * gqb[jm]
            q0[j] = qm if gqb[0] is None else qm * gqb[0]
            kend[j] = km if gkb is None else km * gkb
    g_all = g_mid[0]
    for mi in range(1, n_mid):
        g_all = g_all * g_mid[mi]

    def rows(pieces):
        return jnp.concatenate(pieces, axis=0)

    def slots(groups):
        return jnp.concatenate([rows(p) for p in groups], axis=1).astype(BF16)

    s1 = _dot_nt(slots(q1), slots(k1))
    s2 = _dot_nt(slots(q2), slots(k2))
    scores = (s1 + jnp.where(same_mid, s2 + jnp.where(same_sub, s3, 0.0), 0.0)).astype(BF16)
    o = _dot_nt(rows(q0).astype(BF16), st.astype(BF16)) + jnp.dot(scores, vb, preferred_element_type=F32)

    w, vd = k3, v3
    band = jnp.sum(q3 * w, axis=-1, keepdims=True) * vd
    for d in range(1, hb):
        w = pltpu.roll(w, 1, 1) * f3
        vd = pltpu.roll(vd, 1, 1)
        sd = jnp.sum(q3 * w, axis=-1, keepdims=True)
        band = band + jnp.where(sub_h >= d, sd, 0.0) * vd
    o = o + band.reshape(C, HEAD_DIM)

    st_new = st * g_all + _dot_tn(vb, rows(kend).astype(BF16))

    ms = jnp.mean(o * o, axis=-1, keepdims=True)
    y = o * lax.rsqrt(ms + EPS) * ng * gate
    return y, st_new


def _mix_merge_kernel(q_ref, lf_ref, i_ref, g_ref, u_ref, v_ref, ng_ref, ws_ref, bs_ref,
                      ga_ref, gb_ref, wa_ref, wb_ref, wo_ref, x_ref, gt_ref, *rest, tiles_per_seq, n_cast):
    cast_in, x1_ref, cast_out = rest[:n_cast], rest[n_cast], rest[n_cast + 1:2 * n_cast + 1]
    st_ref, oa_ref, ob_ref = rest[2 * n_cast + 1:]
    for src, dst in zip(cast_in, cast_out):
        dst[...] = src[...].astype(dst.dtype)

    C = HG_CHUNK
    n_heads = st_ref.shape[0]
    n_groups = ws_ref.shape[0]
    tm, D = x_ref.shape

    @pl.when(lax.rem(pl.program_id(0), tiles_per_seq) == 0)
    def _():
        st_ref[...] = jnp.zeros_like(st_ref)

    r_i = lax.broadcasted_iota(jnp.int32, (C, C), 0)
    c_i = lax.broadcasted_iota(jnp.int32, (C, C), 1)
    same_mid = (r_i // HG_MID) == (c_i // HG_MID)
    same_sub = (r_i // HG_SUB) == (c_i // HG_SUB)
    sub = lax.broadcasted_iota(jnp.int32, (C // HG_SUB, HG_SUB, 1), 1)
    tril = jnp.where(r_i >= c_i, 1.0, 0.0)
    ng = ng_ref[...]
    ws = [(ws_ref[g] * tril).astype(BF16) for g in range(n_groups)]
    for ci in range(tm // C):
        rows = pl.ds(ci * C, C)
        for h in range(n_heads):
            sl = pl.ds(h * HEAD_DIM, HEAD_DIM)
            y, st_new = _hgrn_head(
                q_ref[rows, sl].astype(F32), lf_ref[rows, sl].astype(F32), i_ref[rows, sl].astype(F32),
                g_ref[rows, sl].astype(F32), ng, st_ref[h], sub, same_mid, same_sub)
            st_ref[h] = st_new
            oa_ref[rows, sl] = y.astype(BF16)
        for g in range(n_groups):
            cols = pl.ds(g * GROUP_DIM, GROUP_DIM)
            sg = jnp.dot(ws[g], v_ref[rows, cols], preferred_element_type=F32) + bs_ref[:, g:g + 1]
            ob_ref[rows, cols] = (u_ref[rows, cols].astype(F32) * sg).astype(BF16)

    acc = None
    for c in range(D // MERGE_TK):
        cs = pl.ds(c * MERGE_TK, MERGE_TK)
        ya = jnp.dot(oa_ref[...], wa_ref[:, cs], preferred_element_type=F32)
        yb = jnp.dot(ob_ref[...], wb_ref[:, cs], preferred_element_type=F32)
        m = jax.nn.sigmoid(ga_ref[:, cs].astype(F32)) * ya + jax.nn.sigmoid(gb_ref[:, cs].astype(F32)) * yb
        part = jnp.dot(m.astype(BF16), wo_ref[cs, :], preferred_element_type=F32)
        acc = part if acc is None else acc + part
    x1_ref[...] = x_ref[...] + gt_ref[...] * acc


def _side_castable(w, n_steps):
    return w.shape[1] % (n_steps * 16) == 0


def _mix_merge(x, p, norm_g, w_s, b_s, wa, wb, wo, l, mod, to_cast, *, seq, key_width, sgu_width):
    N, D = x.shape
    G, C, _ = w_s.shape
    assert key_width == sgu_width and C == SGU_CHUNK and G * GROUP_DIM == sgu_width and D == 2 * key_width
    kw = key_width
    tm = min(MIX_ROWS, seq)
    assert tm % HG_CHUNK == 0 and seq % tm == 0
    n_steps = N // tm
    per_b = seq // tm
    n_heads = kw // HEAD_DIM
    col = lambda c: (lambda s: (s, c))
    resident = pl.Buffered(1)
    slab = lambda w: w.shape[1] // n_steps
    cast_in_specs = [pl.BlockSpec((None, slab(w), w.shape[2]), lambda s: (l + 1, s, 0)) for w in to_cast]
    cast_out_specs = [pl.BlockSpec((slab(w), w.shape[2]), lambda s: (s, 0)) for w in to_cast]
    cast_out_shapes = [jax.ShapeDtypeStruct(w.shape[1:], BF16) for w in to_cast]
    out = pl.pallas_call(
        functools.partial(_mix_merge_kernel, tiles_per_seq=per_b, n_cast=len(to_cast)),
        grid=(n_steps,),
        in_specs=[
            pl.BlockSpec((tm, kw), col(0)),
            pl.BlockSpec((tm, kw), col(1)),
            pl.BlockSpec((tm, kw), col(2)),
            pl.BlockSpec((tm, kw), col(3)),
            pl.BlockSpec((tm, kw), col(4)),
            pl.BlockSpec((tm, kw), col(5)),
            pl.BlockSpec((1, HEAD_DIM), lambda s: (0, 0)),
            pl.BlockSpec((G, C, C), lambda s: (0, 0, 0)),
            pl.BlockSpec((C, G), lambda s: (0, 0)),
            pl.BlockSpec((tm, D), col(3)),
            pl.BlockSpec((tm, D), col(4)),
            pl.BlockSpec((None, kw, D), lambda s: (l, 0, 0), pipeline_mode=resident),
            pl.BlockSpec((None, kw, D), lambda s: (l, 0, 0), pipeline_mode=resident),
            pl.BlockSpec((None, D, D), lambda s: (l, 0, 0), pipeline_mode=resident),
            pl.BlockSpec((tm, D), col(0)),
            pl.BlockSpec((None, 1, D), lambda s: ((s // per_b) * N_MOD + 2, 0, 0)),
        ] + cast_in_specs,
        out_specs=[pl.BlockSpec((tm, D), col(0))] + cast_out_specs,
        out_shape=[jax.ShapeDtypeStruct((N, D), F32)] + cast_out_shapes,
        scratch_shapes=[
            pltpu.VMEM((n_heads, HEAD_DIM, HEAD_DIM), F32),
            pltpu.VMEM((tm, kw), BF16),
            pltpu.VMEM((tm, kw), BF16),
        ],
        compiler_params=_cparams(("arbitrary",)),
        name="mixers_merge",
    )(p, p, p, p, p, p, norm_g.reshape(1, HEAD_DIM), w_s, b_s.T, p, p, wa, wb, wo, x, mod, *to_cast)
    return out[0], list(out[1:])


def _ffn_kernel(x_ref, g_ref, sh_ref, sc_ref, gt_ref, w1_ref, w2_ref, fg_ref, o_ref, h_ref, *, final_norm):
    kk = pl.program_id(1)

    def gated_chunk(h):
        hid = jnp.dot(h, w1_ref[...], preferred_element_type=F32)
        act = jnp.square(jnp.maximum(hid, 0.0)).astype(BF16)
        return gt_ref[...] * jnp.dot(act, w2_ref[...], preferred_element_type=F32)

    @pl.when(kk == 0)
    def _():
        x = x_ref[...]
        h = _modnorm(x, g_ref[...], sh_ref[...], sc_ref[...]).astype(BF16)
        h_ref[...] = h
        o_ref[...] = x + gated_chunk(h)

    @pl.when(kk > 0)
    def _():
        o_ref[...] += gated_chunk(h_ref[...])

    if final_norm:
        @pl.when(kk == pl.num_programs(1) - 1)
        def _():
            x2 = o_ref[...]
            ms = jnp.mean(x2 * x2, axis=-1, keepdims=True)
            o_ref[...] = x2 * lax.rsqrt(ms + EPS) * fg_ref[...]


def _ffn(x1, w1, w2, mod, norm2_g, final_g, *, seq, final_norm):
    N, D = x1.shape
    FF = w1.shape[1]
    th = min(FFN_TH, FF)
    tm = min(1024, seq)
    per_b = seq // tm
    modspec = lambda which: pl.BlockSpec((None, 1, D), lambda i, k: ((i // per_b) * N_MOD + which, 0, 0))
    return pl.pallas_call(
        functools.partial(_ffn_kernel, final_norm=final_norm),
        grid=(N // tm, FF // th),
        in_specs=[
            pl.BlockSpec((tm, D), lambda i, k: (i, 0)),
            pl.BlockSpec((1, D), lambda i, k: (0, 0)),
            modspec(3),
            modspec(4),
            modspec(5),
            pl.BlockSpec((D, th), lambda i, k: (0, k)),
            pl.BlockSpec((th, D), lambda i, k: (k, 0)),
            pl.BlockSpec((1, D), lambda i, k: (0, 0)),
        ],
        out_specs=pl.BlockSpec((tm, D), lambda i, k: (i, 0)),
        out_shape=jax.ShapeDtypeStruct((N, D), F32),
        scratch_shapes=[pltpu.VMEM((tm, D), BF16)],
        compiler_params=_cparams(("parallel", "arbitrary")),
        name="relu2_mlp",
    )(x1, norm2_g.reshape(1, D), mod, mod, mod, w1, w2, final_g.reshape(1, D))


def kernel(x, c, w_ada, b_ada, norm1_g, norm2_g, w_in, hgrn_lower_bounds, hgrn_norm_g, ln_v_g, ln_v_b,
           w_spatial, b_spatial, w_proj_a, w_proj_b, w_out, w_ff_in, w_ff_out, final_norm_g):
    B, T, D = x.shape
    L = w_ada.shape[0]
    N = B * T
    key_width = hgrn_lower_bounds.shape[1]
    sgu_width = w_proj_b.shape[1]
    assert w_proj_a.shape[1] == key_width

    mod_all = _modulation(c, w_ada, b_ada)
    lbs = _lower_bounds(hgrn_lower_bounds)
    wa_b, wb_b, wo_b = w_proj_a.astype(BF16), w_proj_b.astype(BF16), w_out.astype(BF16)
    big = (w_in, w_ff_in, w_ff_out)
    side_cast = all(_side_castable(w, N // min(MIX_ROWS, T)) for w in big)
    w_in_l, w1_l, w2_l = (w[0].astype(BF16) for w in big)

    xf = x.reshape(N, D)
    for l in range(L):
        mod = mod_all[l].reshape(B * N_MOD, 1, D)
        p = _inproj(xf, mod, norm1_g[l], w_in_l, lbs[l], ln_v_g[l], ln_v_b[l], seq=T)
        nxt = big if (side_cast and l + 1 < L) else ()
        x1, cast = _mix_merge(xf, p, hgrn_norm_g[l], w_spatial[l], b_spatial[l], wa_b, wb_b, wo_b, l, mod, nxt,
                              seq=T, key_width=key_width, sgu_width=sgu_width)
        xf = _ffn(x1, w1_l, w2_l, mod, norm2_g[l], final_norm_g, seq=T, final_norm=(l == L - 1))
        if l + 1 < L:
            w_in_l, w1_l, w2_l = cast if side_cast else (w[l + 1].astype(BF16) for w in big)
    return xf.reshape(B, T, D)
```

```python
import functools

import jax
import jax.numpy as jnp
from jax import lax
from jax.experimental import pallas as pl
from jax.experimental.pallas import tpu as pltpu

F32 = jnp.float32
BF16 = jnp.bfloat16

EPS = 1e-6
MIN_FORGET = 1e-20
N_MOD = 6
HEAD_DIM = 128
GROUP_DIM = 128
SGU_CHUNK = 128
HG_CHUNK = 128
HG_MID = 32
HG_SUB = 8
MIX_ROWS = 256
VMEM_LIMIT = 60 * 1024 * 1024
INPROJ_TN = 2048
MERGE_TK = 1024
FFN_TH = 1024


def _cparams(sem):
    return pltpu.CompilerParams(dimension_semantics=sem, vmem_limit_bytes=VMEM_LIMIT)


def _silu(x):
    return x * jax.nn.sigmoid(x)


def _gelu(x):
    return 0.5 * x * (1.0 + lax.erf(x * (0.5 ** 0.5)))


def _modnorm(x, g, shift, scale):
    ms = jnp.mean(x * x, axis=-1, keepdims=True)
    y = x * lax.rsqrt(ms + EPS) * g
    return y * (1.0 + scale) + shift


def _mod_kernel(c_ref, w_ref, b_ref, o_ref):
    c = c_ref[...]
    ca = _silu(c).astype(BF16)
    o_ref[...] = jnp.dot(ca, w_ref[...].astype(BF16), preferred_element_type=F32) + b_ref[...]


def _modulation(c, w_ada, b_ada):
    L, D, W = w_ada.shape
    B = c.shape[0]
    tn = min(2048, W)
    return pl.pallas_call(
        _mod_kernel,
        grid=(L, W // tn),
        in_specs=[
            pl.BlockSpec((B, D), lambda l, j: (0, 0)),
            pl.BlockSpec((None, D, tn), lambda l, j: (l, 0, j)),
            pl.BlockSpec((None, 1, tn), lambda l, j: (l, 0, j)),
        ],
        out_specs=pl.BlockSpec((None, B, tn), lambda l, j: (l, 0, j)),
        out_shape=jax.ShapeDtypeStruct((L, B, W), F32),
        compiler_params=_cparams(("arbitrary", "arbitrary")),
        name="adaln_modulation",
    )(c, w_ada, b_ada.reshape(L, 1, W))


def _lower_bounds_kernel(lb_ref, o_ref):
    x = lb_ref[...]
    m = jnp.max(x, axis=0, keepdims=True)
    e = jnp.exp(x - m)
    p = e / jnp.sum(e, axis=0, keepdims=True)
    L = x.shape[0]
    acc = jnp.zeros_like(p[0:1])
    rows = []
    for l in range(L):
        acc = acc + p[l:l + 1]
        rows.append(acc - p[0:1])
    o_ref[...] = jnp.concatenate(rows, axis=0)


def _lower_bounds(lb):
    return pl.pallas_call(
        _lower_bounds_kernel,
        out_shape=jax.ShapeDtypeStruct(lb.shape, F32),
        name="hgrn_lower_bounds",
    )(lb)


def _inproj_kernel(x_ref, sh_ref, sc_ref, g_ref, w_ref, lb_ref, lng_ref, lnb_ref, p_ref, h_ref):
    j = pl.program_id(1)
    half = w_ref.shape[1] // 2

    def project():
        return jnp.dot(h_ref[...], w_ref[...], preferred_element_type=F32)

    @pl.when(j == 0)
    def _():
        h_ref[...] = _modnorm(x_ref[...], g_ref[...], sh_ref[...], sc_ref[...]).astype(BF16)
        acc = project()
        lb = lb_ref[...]
        f = lb + (1.0 - lb) * jax.nn.sigmoid(acc[:, half:])
        p_ref[:, :half] = _silu(acc[:, :half]).astype(p_ref.dtype)
        p_ref[:, half:] = jnp.log(jnp.maximum(f, MIN_FORGET)).astype(p_ref.dtype)

    @pl.when(j == 1)
    def _():
        acc = project()
        p_ref[:, :half] = acc[:, :half].astype(p_ref.dtype)
        p_ref[:, half:] = _silu(acc[:, half:]).astype(p_ref.dtype)

    @pl.when(j == 2)
    def _():
        acc = project()
        p_ref[:, :half] = _gelu(acc[:, :half]).astype(p_ref.dtype)
        vg = _gelu(acc[:, half:])
        mu = jnp.mean(vg, axis=-1, keepdims=True)
        vc = vg - mu
        var = jnp.mean(vc * vc, axis=-1, keepdims=True)
        p_ref[:, half:] = (vc * lax.rsqrt(var + EPS) * lng_ref[...] + lnb_ref[...]).astype(p_ref.dtype)

    @pl.when(j >= 3)
    def _():
        p_ref[...] = project().astype(p_ref.dtype)


def _inproj(x, mod, norm_g, w, lb, ln_g, ln_b, *, seq):
    N, D = x.shape
    W = w.shape[1]
    tn = INPROJ_TN
    half = tn // 2
    assert lb.shape[-1] == half and ln_g.shape[-1] == half and W == 3 * tn + 2 * D and D == tn
    tm = min(1024, seq)
    per_b = seq // tm
    return pl.pallas_call(
        _inproj_kernel,
        grid=(N // tm, W // tn),
        in_specs=[
            pl.BlockSpec((tm, D), lambda i, j: (i, 0)),
            pl.BlockSpec((None, 1, D), lambda i, j: ((i // per_b) * N_MOD + 0, 0, 0)),
            pl.BlockSpec((None, 1, D), lambda i, j: ((i // per_b) * N_MOD + 1, 0, 0)),
            pl.BlockSpec((1, D), lambda i, j: (0, 0)),
            pl.BlockSpec((D, tn), lambda i, j: (0, j)),
            pl.BlockSpec((1, half), lambda i, j: (0, 0)),
            pl.BlockSpec((1, half), lambda i, j: (0, 0)),
            pl.BlockSpec((1, half), lambda i, j: (0, 0)),
        ],
        out_specs=pl.BlockSpec((tm, tn), lambda i, j: (i, j)),
        out_shape=jax.ShapeDtypeStruct((N, W), BF16),
        scratch_shapes=[pltpu.VMEM((tm, D), BF16)],
        compiler_params=_cparams(("parallel", "arbitrary")),
        name="in_projection",
    )(x, mod, mod, norm_g.reshape(1, D), w, lb.reshape(1, half), ln_g.reshape(1, half), ln_b.reshape(1, half))


def _dot_nt(a, b):
    return lax.dot_general(a, b, (((1,), (1,)), ((), ())), preferred_element_type=F32)


def _dot_tn(a, b):
    return lax.dot_general(a, b, (((0,), (0,)), ((), ())), preferred_element_type=F32)


def _hgrn_pairs(qs, lf, vr, sub, same_mid, same_sub):
    C = HG_CHUNK
    nb = C // HG_SUB
    n_mid = C // HG_MID
    n_sub = HG_MID // HG_SUB
    hb = HG_SUB // 2
    shp3 = (nb, HG_SUB, HEAD_DIM)

    q3 = qs.reshape(shp3)
    v3 = vr.reshape(shp3)
    vb = vr.astype(BF16)
    hi = sub >= hb
    sub_h = jnp.where(hi, sub - hb, sub)

    a = lf.reshape(shp3)
    f3 = jnp.exp(a)
    k3 = 1.0 - f3
    for sh in (1, 2):
        a = a + jnp.where(sub_h >= sh, pltpu.roll(a, sh, 1), 0.0)
    e4 = jnp.exp(a)
    tot = jnp.where(hi, a[:, HG_SUB - 1:HG_SUB, :], a[:, hb - 1:hb, :])
    qe4 = q3 * e4
    ke4 = k3 * jnp.exp(tot - a)
    g_lo, g_hi = e4[:, hb - 1:hb, :], e4[:, HG_SUB - 1:HG_SUB, :]
    s3 = _dot_nt(jnp.where(hi, qe4, 0.0).reshape(C, HEAD_DIM).astype(BF16),
                 jnp.where(hi, 0.0, ke4).reshape(C, HEAD_DIM).astype(BF16))
    qe = jnp.where(hi, qe4 * g_lo, qe4)
    ke = jnp.where(hi, ke4, ke4 * g_hi)
    g8 = g_lo * g_hi
    g = [g8[j] for j in range(nb)]

    def bc(r):
        return jnp.broadcast_to(r, (HG_SUB, HEAD_DIM))

    pre, suf, g_mid = [], [], []
    for mi in range(n_mid):
        gs = g[mi * n_sub:(mi + 1) * n_sub]
        p = [None]
        for i in range(1, n_sub):
            p.append(gs[i - 1] if p[-1] is None else p[-1] * gs[i - 1])
        s = [None] * n_sub
        for i in range(n_sub - 2, -1, -1):
            s[i] = gs[i + 1] if s[i + 1] is None else s[i + 1] * gs[i + 1]
        pre.append(p)
        suf.append(s)
        g_mid.append(p[-1] * gs[-1])

    zero = jnp.zeros((HG_SUB, HEAD_DIM), F32)
    q2 = [[zero] * nb for _ in range(n_sub - 1)]
    k2 = [[zero] * nb for _ in range(n_sub - 1)]
    q1 = [[zero] * nb for _ in range(n_mid - 1)]
    k1 = [[zero] * nb for _ in range(n_mid - 1)]
    q0, kend = [None] * nb, [None] * nb
    for mi in range(n_mid):
        gq = [None] * (mi + 1)
        for jm in range(mi - 1, -1, -1):
            gq[jm] = g_mid[jm] if gq[jm + 1] is None else gq[jm + 1] * g_mid[jm]
        gqb = [None if r is None else bc(r) for r in gq]
        gk = None
        for m2 in range(mi + 1, n_mid):
            gk = g_mid[m2] if gk is None else gk * g_mid[m2]
        gkb = None if gk is None else bc(gk)
        for i in range(n_sub):
            j = mi * n_sub + i
            if i < n_sub - 1:
                k2[i][j] = ke[j]
            m = qe[j]
            for jj in range(i, 0, -1):
                q2[jj - 1][j] = m
                if jj > 1:
                    m = m * bc(g[mi * n_sub + jj - 1])
            qm = qe[j] if pre[mi][i] is None else qe[j] * bc(pre[mi][i])
            km = ke[j] if suf[mi][i] is None else ke[j] * bc(suf[mi][i])
            if mi < n_mid - 1:
                k1[mi][j] = km
            for jm in range(1, mi + 1):
                q1[jm - 1][j] = qm if gqb[jm] is None else qm * gqb[jm]
            q0[j] = qm if gqb[0] is None else qm * gqb[0]
            kend[j] = km if gkb is None else km * gkb
    g_all = g_mid[0]
    for mi in range(1, n_mid):
        g_all = g_all * g_mid[mi]

    def rows(pieces):
        return jnp.concatenate(pieces, axis=0)

    def slots(groups):
        return jnp.concatenate([rows(p) for p in groups], axis=1).astype(BF16)

    s1 = _dot_nt(slots(q1), slots(k1))
    s2 = _dot_nt(slots(q2), slots(k2))
    scores = (s1 + jnp.where(same_mid, s2 + jnp.where(same_sub, s3, 0.0), 0.0)).astype(BF16)

    w, vd = k3, v3
    band = jnp.sum(q3 * w, axis=-1, keepdims=True) * vd
    for d in range(1, hb):
        w = pltpu.roll(w, 1, 1) * f3
        vd = pltpu.roll(vd, 1, 1)
        sd = jnp.sum(q3 * w, axis=-1, keepdims=True)
        band = band + jnp.where(sub_h >= d, sd, 0.0) * vd
    return scores, band.reshape(C, HEAD_DIM), rows(q0).astype(BF16), rows(kend).astype(BF16), vb, g_all


def _hgrn_state(pairs, st):
    scores, band, q_st, k_end, vb, g_all = pairs
    o = _dot_nt(q_st, st.astype(BF16)) + jnp.dot(scores, vb, preferred_element_type=F32) + band
    return o, st * g_all + _dot_tn(vb, k_end)


def _hgrn_out(o, gate, ng):
    ms = jnp.mean(o * o, axis=-1, keepdims=True)
    return o * lax.rsqrt(ms + EPS) * ng * gate


def _mix_merge_kernel(q_ref, lf_ref, i_ref, g_ref, u_ref, v_ref, ng_ref, ws_ref, bs_ref,
                      ga_ref, gb_ref, wa_ref, wb_ref, wo_ref, x_ref, gt_ref, *rest, tiles_per_seq, n_cast):
    cast_in, x1_ref, cast_out = rest[:n_cast], rest[n_cast], rest[n_cast + 1:2 * n_cast + 1]
    st_ref, oa_ref, ob_ref = rest[2 * n_cast + 1:]
    for src, dst in zip(cast_in, cast_out):
        dst[...] = src[...].astype(dst.dtype)

    C = HG_CHUNK
    n_heads = st_ref.shape[0]
    n_groups = ws_ref.shape[0]
    tm, D = x_ref.shape

    @pl.when(lax.rem(pl.program_id(0), tiles_per_seq) == 0)
    def _():
        st_ref[...] = jnp.zeros_like(st_ref)

    r_i = lax.broadcasted_iota(jnp.int32, (C, C), 0)
    c_i = lax.broadcasted_iota(jnp.int32, (C, C), 1)
    same_mid = (r_i // HG_MID) == (c_i // HG_MID)
    same_sub = (r_i // HG_SUB) == (c_i // HG_SUB)
    sub = lax.broadcasted_iota(jnp.int32, (C // HG_SUB, HG_SUB, 1), 1)
    tril = jnp.where(r_i >= c_i, 1.0, 0.0)
    ng = ng_ref[...]
    ws = [(ws_ref[g] * tril).astype(BF16) for g in range(n_groups)]
    heads = [pl.ds(h * HEAD_DIM, HEAD_DIM) for h in range(n_heads)]
    for ci in range(tm // C):
        rows = pl.ds(ci * C, C)
        pairs = [_hgrn_pairs(q_ref[rows, sl].astype(F32), lf_ref[rows, sl].astype(F32),
                             i_ref[rows, sl].astype(F32), sub, same_mid, same_sub) for sl in heads]
        outs = []
        for h in range(n_heads):
            o, st_ref[h] = _hgrn_state(pairs[h], st_ref[h])
            outs.append(o)
        for sl, o in zip(heads, outs):
            oa_ref[rows, sl] = _hgrn_out(o, g_ref[rows, sl].astype(F32), ng).astype(BF16)
        for g in range(n_groups):
            cols = pl.ds(g * GROUP_DIM, GROUP_DIM)
            sg = jnp.dot(ws[g], v_ref[rows, cols], preferred_element_type=F32) + bs_ref[:, g:g + 1]
            ob_ref[rows, cols] = (u_ref[rows, cols].astype(F32) * sg).astype(BF16)

    acc = None
    for c in range(D // MERGE_TK):
        cs = pl.ds(c * MERGE_TK, MERGE_TK)
        ya = jnp.dot(oa_ref[...], wa_ref[:, cs], preferred_element_type=F32)
        yb = jnp.dot(ob_ref[...], wb_ref[:, cs], preferred_element_type=F32)
        m = jax.nn.sigmoid(ga_ref[:, cs].astype(F32)) * ya + jax.nn.sigmoid(gb_ref[:, cs].astype(F32)) * yb
        part = jnp.dot(m.astype(BF16), wo_ref[cs, :], preferred_element_type=F32)
        acc = part if acc is None else acc + part
    x1_ref[...] = x_ref[...] + gt_ref[...] * acc


def _side_castable(w, n_steps):
    return w.shape[1] % (n_steps * 16) == 0


def _mix_merge(x, p, norm_g, w_s, b_s, wa, wb, wo, l, mod, to_cast, *, seq, key_width, sgu_width):
    N, D = x.shape
    G, C, _ = w_s.shape
    assert key_width == sgu_width and C == SGU_CHUNK and G * GROUP_DIM == sgu_width and D == 2 * key_width
    kw = key_width
    tm = min(MIX_ROWS, seq)
    assert tm % HG_CHUNK == 0 and seq % tm == 0
    n_steps = N // tm
    per_b = seq // tm
    n_heads = kw // HEAD_DIM
    col = lambda c: (lambda s: (s, c))
    resident = pl.Buffered(1)
    slab = lambda w: w.shape[1] // n_steps
    cast_in_specs = [pl.BlockSpec((None, slab(w), w.shape[2]), lambda s: (l + 1, s, 0)) for w in to_cast]
    cast_out_specs = [pl.BlockSpec((slab(w), w.shape[2]), lambda s: (s, 0)) for w in to_cast]
    cast_out_shapes = [jax.ShapeDtypeStruct(w.shape[1:], BF16) for w in to_cast]
    out = pl.pallas_call(
        functools.partial(_mix_merge_kernel, tiles_per_seq=per_b, n_cast=len(to_cast)),
        grid=(n_steps,),
        in_specs=[
            pl.BlockSpec((tm, kw), col(0)),
            pl.BlockSpec((tm, kw), col(1)),
            pl.BlockSpec((tm, kw), col(2)),
            pl.BlockSpec((tm, kw), col(3)),
            pl.BlockSpec((tm, kw), col(4)),
            pl.BlockSpec((tm, kw), col(5)),
            pl.BlockSpec((1, HEAD_DIM), lambda s: (0, 0)),
            pl.BlockSpec((G, C, C), lambda s: (0, 0, 0)),
            pl.BlockSpec((C, G), lambda s: (0, 0)),
            pl.BlockSpec((tm, D), col(3)),
            pl.BlockSpec((tm, D), col(4)),
            pl.BlockSpec((None, kw, D), lambda s: (l, 0, 0), pipeline_mode=resident),
            pl.BlockSpec((None, kw, D), lambda s: (l, 0, 0), pipeline_mode=resident),
            pl.BlockSpec((None, D, D), lambda s: (l, 0, 0), pipeline_mode=resident),
            pl.BlockSpec((tm, D), col(0)),
            pl.BlockSpec((None, 1, D), lambda s: ((s // per_b) * N_MOD + 2, 0, 0)),
        ] + cast_in_specs,
        out_specs=[pl.BlockSpec((tm, D), col(0))] + cast_out_specs,
        out_shape=[jax.ShapeDtypeStruct((N, D), F32)] + cast_out_shapes,
        scratch_shapes=[
            pltpu.VMEM((n_heads, HEAD_DIM, HEAD_DIM), F32),
            pltpu.VMEM((tm, kw), BF16),
            pltpu.VMEM((tm, kw), BF16),
        ],
        compiler_params=_cparams(("arbitrary",)),
        name="mixers_merge",
    )(p, p, p, p, p, p, norm_g.reshape(1, HEAD_DIM), w_s, b_s.T, p, p, wa, wb, wo, x, mod, *to_cast)
    return out[0], list(out[1:])


def _ffn_kernel(x_ref, g_ref, sh_ref, sc_ref, gt_ref, w1_ref, w2_ref, fg_ref, o_ref, h_ref, *, final_norm):
    kk = pl.program_id(1)

    def gated_chunk(h):
        hid = jnp.dot(h, w1_ref[...], preferred_element_type=F32)
        act = jnp.square(jnp.maximum(hid, 0.0)).astype(BF16)
        return gt_ref[...] * jnp.dot(act, w2_ref[...], preferred_element_type=F32)

    @pl.when(kk == 0)
    def _():
        x = x_ref[...]
        h = _modnorm(x, g_ref[...], sh_ref[...], sc_ref[...]).astype(BF16)
        h_ref[...] = h
        o_ref[...] = x + gated_chunk(h)

    @pl.when(kk > 0)
    def _():
        o_ref[...] += gated_chunk(h_ref[...])

    if final_norm:
        @pl.when(kk == pl.num_programs(1) - 1)
        def _():
            x2 = o_ref[...]
            ms = jnp.mean(x2 * x2, axis=-1, keepdims=True)
            o_ref[...] = x2 * lax.rsqrt(ms + EPS) * fg_ref[...]


def _ffn(x1, w1, w2, mod, norm2_g, final_g, *, seq, final_norm):
    N, D = x1.shape
    FF = w1.shape[1]
    th = min(FFN_TH, FF)
    tm = min(1024, seq)
    per_b = seq // tm
    modspec = lambda which: pl.BlockSpec((None, 1, D), lambda i, k: ((i // per_b) * N_MOD + which, 0, 0))
    return pl.pallas_call(
        functools.partial(_ffn_kernel, final_norm=final_norm),
        grid=(N // tm, FF // th),
        in_specs=[
            pl.BlockSpec((tm, D), lambda i, k: (i, 0)),
            pl.BlockSpec((1, D), lambda i, k: (0, 0)),
            modspec(3),
            modspec(4),
            modspec(5),
            pl.BlockSpec((D, th), lambda i, k: (0, k)),
            pl.BlockSpec((th, D), lambda i, k: (k, 0)),
            pl.BlockSpec((1, D), lambda i, k: (0, 0)),
        ],
        out_specs=pl.BlockSpec((tm, D), lambda i, k: (i, 0)),
        out_shape=jax.ShapeDtypeStruct((N, D), F32),
        scratch_shapes=[pltpu.VMEM((tm, D), BF16)],
        compiler_params=_cparams(("parallel", "arbitrary")),
        name="relu2_mlp",
    )(x1, norm2_g.reshape(1, D), mod, mod, mod, w1, w2, final_g.reshape(1, D))


def kernel(x, c, w_ada, b_ada, norm1_g, norm2_g, w_in, hgrn_lower_bounds, hgrn_norm_g, ln_v_g, ln_v_b,
           w_spatial, b_spatial, w_proj_a, w_proj_b, w_out, w_ff_in, w_ff_out, final_norm_g):
    B, T, D = x.shape
    L = w_ada.shape[0]
    N = B * T
    key_width = hgrn_lower_bounds.shape[1]
    sgu_width = w_proj_b.shape[1]
    assert w_proj_a.shape[1] == key_width

    mod_all = _modulation(c, w_ada, b_ada)
    lbs = _lower_bounds(hgrn_lower_bounds)
    wa_b, wb_b, wo_b = w_proj_a.astype(BF16), w_proj_b.astype(BF16), w_out.astype(BF16)
    big = (w_in, w_ff_in, w_ff_out)
    side_cast = all(_side_castable(w, N // min(MIX_ROWS, T)) for w in big)
    w_in_l, w1_l, w2_l = (w[0].astype(BF16) for w in big)

    xf = x.reshape(N, D)
    for l in range(L):
        mod = mod_all[l].reshape(B * N_MOD, 1, D)
        p = _inproj(xf, mod, norm1_g[l], w_in_l, lbs[l], ln_v_g[l], ln_v_b[l], seq=T)
        nxt = big if (side_cast and l + 1 < L) else ()
        x1, cast = _mix_merge(xf, p, hgrn_norm_g[l], w_spatial[l], b_spatial[l], wa_b, wb_b, wo_b, l, mod, nxt,
                              seq=T, key_width=key_width, sgu_width=sgu_width)
        xf = _ffn(x1, w1_l, w2_l, mod, norm2_g[l], final_norm_g, seq=T, final_norm=(l == L - 1))
        if l + 1 < L:
            w_in_l, w1_l, w2_l = cast if side_cast else (w[l + 1].astype(BF16) for w in big)
    return xf.reshape(B, T, D)
```

```python
import functools

import jax
import jax.numpy as jnp
from jax import lax
from jax.experimental import pallas as pl
from jax.experimental.pallas import tpu as pltpu

F32 = jnp.float32
BF16 = jnp.bfloat16

EPS = 1e-6
MIN_FORGET = 1e-20
N_MOD = 6
HEAD_DIM = 128
GROUP_DIM = 128
SGU_CHUNK = 128
HG_CHUNK = 128
HG_MID = 32
HG_SUB = 8
MIX_ROWS = 256
VMEM_LIMIT = 60 * 1024 * 1024
ADA_TN = 2048
ROW_TILE = 1024
INPROJ_TN = 2048
MERGE_TK = 1024
FFN_TH = 1024


def _cparams(sem):
    return pltpu.CompilerParams(dimension_semantics=sem, vmem_limit_bytes=VMEM_LIMIT)


def _silu(x):
    return x * jax.nn.sigmoid(x)


def _gelu(x):
    return 0.5 * x * (1.0 + lax.erf(x * (0.5 ** 0.5)))


def _modnorm(x, g, shift, scale):
    ms = jnp.mean(x * x, axis=-1, keepdims=True)
    y = x * lax.rsqrt(ms + EPS) * g
    return y * (1.0 + scale) + shift


def _mod_kernel(c_ref, w_ref, b_ref, o_ref):
    c = c_ref[...]
    ca = _silu(c).astype(BF16)
    o_ref[...] = jnp.dot(ca, w_ref[...].astype(BF16), preferred_element_type=F32) + b_ref[...]


def _modulation(c, w_ada, b_ada):
    L, D, W = w_ada.shape
    B = c.shape[0]
    tn = min(ADA_TN, W)
    return pl.pallas_call(
        _mod_kernel,
        grid=(L, W // tn),
        in_specs=[
            pl.BlockSpec((B, D), lambda l, j: (0, 0)),
            pl.BlockSpec((None, D, tn), lambda l, j: (l, 0, j)),
            pl.BlockSpec((None, 1, tn), lambda l, j: (l, 0, j)),
        ],
        out_specs=pl.BlockSpec((None, B, tn), lambda l, j: (l, 0, j)),
        out_shape=jax.ShapeDtypeStruct((L, B, W), F32),
        compiler_params=_cparams(("arbitrary", "arbitrary")),
        name="adaln_modulation",
    )(c, w_ada, b_ada.reshape(L, 1, W))


def _lower_bounds_kernel(lb_ref, o_ref):
    x = lb_ref[...]
    m = jnp.max(x, axis=0, keepdims=True)
    e = jnp.exp(x - m)
    p = e / jnp.sum(e, axis=0, keepdims=True)
    L = x.shape[0]
    acc = jnp.zeros_like(p[0:1])
    rows = []
    for l in range(L):
        acc = acc + p[l:l + 1]
        rows.append(acc - p[0:1])
    o_ref[...] = jnp.concatenate(rows, axis=0)


def _lower_bounds(lb):
    return pl.pallas_call(
        _lower_bounds_kernel,
        out_shape=jax.ShapeDtypeStruct(lb.shape, F32),
        name="hgrn_lower_bounds",
    )(lb)


def _inproj_kernel(x_ref, sh_ref, sc_ref, g_ref, w_ref, lb_ref, lng_ref, lnb_ref, p_ref, h_ref):
    j = pl.program_id(1)
    half = w_ref.shape[1] // 2

    def project():
        return jnp.dot(h_ref[...], w_ref[...], preferred_element_type=F32)

    @pl.when(j == 0)
    def _():
        h_ref[...] = _modnorm(x_ref[...], g_ref[...], sh_ref[...], sc_ref[...]).astype(BF16)
        acc = project()
        lb = lb_ref[...]
        f = lb + (1.0 - lb) * jax.nn.sigmoid(acc[:, half:])
        p_ref[:, :half] = _silu(acc[:, :half]).astype(p_ref.dtype)
        p_ref[:, half:] = jnp.log(jnp.maximum(f, MIN_FORGET)).astype(p_ref.dtype)

    @pl.when(j == 1)
    def _():
        acc = project()
        p_ref[:, :half] = acc[:, :half].astype(p_ref.dtype)
        p_ref[:, half:] = _silu(acc[:, half:]).astype(p_ref.dtype)

    @pl.when(j == 2)
    def _():
        acc = project()
        p_ref[:, :half] = _gelu(acc[:, :half]).astype(p_ref.dtype)
        vg = _gelu(acc[:, half:])
        mu = jnp.mean(vg, axis=-1, keepdims=True)
        vc = vg - mu
        var = jnp.mean(vc * vc, axis=-1, keepdims=True)
        p_ref[:, half:] = (vc * lax.rsqrt(var + EPS) * lng_ref[...] + lnb_ref[...]).astype(p_ref.dtype)

    @pl.when(j >= 3)
    def _():
        p_ref[...] = project().astype(p_ref.dtype)


def _inproj(x, mod, norm_g, w, lb, ln_g, ln_b, *, seq):
    N, D = x.shape
    W = w.shape[1]
    tn = INPROJ_TN
    half = tn // 2
    assert lb.shape[-1] == half and ln_g.shape[-1] == half and W == 3 * tn + 2 * D and D == tn
    tm = min(ROW_TILE, seq)
    per_b = seq // tm
    return pl.pallas_call(
        _inproj_kernel,
        grid=(N // tm, W // tn),
        in_specs=[
            pl.BlockSpec((tm, D), lambda i, j: (i, 0)),
            pl.BlockSpec((None, 1, D), lambda i, j: ((i // per_b) * N_MOD + 0, 0, 0)),
            pl.BlockSpec((None, 1, D), lambda i, j: ((i // per_b) * N_MOD + 1, 0, 0)),
            pl.BlockSpec((1, D), lambda i, j: (0, 0)),
            pl.BlockSpec((D, tn), lambda i, j: (0, j)),
            pl.BlockSpec((1, half), lambda i, j: (0, 0)),
            pl.BlockSpec((1, half), lambda i, j: (0, 0)),
            pl.BlockSpec((1, half), lambda i, j: (0, 0)),
        ],
        out_specs=pl.BlockSpec((tm, tn), lambda i, j: (i, j)),
        out_shape=jax.ShapeDtypeStruct((N, W), BF16),
        scratch_shapes=[pltpu.VMEM((tm, D), BF16)],
        compiler_params=_cparams(("parallel", "arbitrary")),
        name="in_projection",
    )(x, mod, mod, norm_g.reshape(1, D), w, lb.reshape(1, half), ln_g.reshape(1, half), ln_b.reshape(1, half))


def _dot_nt(a, b):
    return lax.dot_general(a, b, (((1,), (1,)), ((), ())), preferred_element_type=F32)


def _dot_tn(a, b):
    return lax.dot_general(a, b, (((0,), (0,)), ((), ())), preferred_element_type=F32)


def _hgrn_pairs(qs, lf, vb, sub, same_mid, same_sub):
    C = HG_CHUNK
    nb = C // HG_SUB
    n_mid = C // HG_MID
    n_sub = HG_MID // HG_SUB
    hb = HG_SUB // 2
    shp3 = (nb, HG_SUB, HEAD_DIM)

    q3 = qs.reshape(shp3)
    v3 = vb.astype(F32).reshape(shp3)
    hi = sub >= hb
    sub_h = jnp.where(hi, sub - hb, sub)

    a = lf.reshape(shp3)
    f3 = jnp.exp(a)
    k3 = 1.0 - f3
    for sh in (1, 2):
        a = a + jnp.where(sub_h >= sh, pltpu.roll(a, sh, 1), 0.0)
    e4 = jnp.exp(a)
    tot = jnp.where(hi, a[:, HG_SUB - 1:HG_SUB, :], a[:, hb - 1:hb, :])
    qe4 = q3 * e4
    ke4 = k3 * jnp.exp(tot - a)
    g_lo, g_hi = e4[:, hb - 1:hb, :], e4[:, HG_SUB - 1:HG_SUB, :]
    s3 = _dot_nt(jnp.where(hi, qe4, 0.0).reshape(C, HEAD_DIM).astype(BF16),
                 jnp.where(hi, 0.0, ke4).reshape(C, HEAD_DIM).astype(BF16))
    qe = jnp.where(hi, qe4 * g_lo, qe4)
    ke = jnp.where(hi, ke4, ke4 * g_hi)
    g8 = g_lo * g_hi
    g = [g8[j] for j in range(nb)]

    def bc(r):
        return jnp.broadcast_to(r, (HG_SUB, HEAD_DIM))

    pre, suf, g_mid = [], [], []
    for mi in range(n_mid):
        gs = g[mi * n_sub:(mi + 1) * n_sub]
        p = [None]
        for i in range(1, n_sub):
            p.append(gs[i - 1] if p[-1] is None else p[-1] * gs[i - 1])
        s = [None] * n_sub
        for i in range(n_sub - 2, -1, -1):
            s[i] = gs[i + 1] if s[i + 1] is None else s[i + 1] * gs[i + 1]
        pre.append(p)
        suf.append(s)
        g_mid.append(p[-1] * gs[-1])

    zero = jnp.zeros((HG_SUB, HEAD_DIM), F32)
    q2 = [[zero] * nb for _ in range(n_sub - 1)]
    k2 = [[zero] * nb for _ in range(n_sub - 1)]
    q1 = [[zero] * nb for _ in range(n_mid - 1)]
    k1 = [[zero] * nb for _ in range(n_mid - 1)]
    q0, kend = [None] * nb, [None] * nb
    for mi in range(n_mid):
        gq = [None] * (mi + 1)
        for jm in range(mi - 1, -1, -1):
            gq[jm] = g_mid[jm] if gq[jm + 1] is None else gq[jm + 1] * g_mid[jm]
        gqb = [None if r is None else bc(r) for r in gq]
        gk = None
        for m2 in range(mi + 1, n_mid):
            gk = g_mid[m2] if gk is None else gk * g_mid[m2]
        gkb = None if gk is None else bc(gk)
        for i in range(n_sub):
            j = mi * n_sub + i
            if i < n_sub - 1:
                k2[i][j] = ke[j]
            m = qe[j]
            for jj in range(i, 0, -1):
                q2[jj - 1][j] = m
                if jj > 1:
                    m = m * bc(g[mi * n_sub + jj - 1])
            qm = qe[j] if pre[mi][i] is None else qe[j] * bc(pre[mi][i])
            km = ke[j] if suf[mi][i] is None else ke[j] * bc(suf[mi][i])
            if mi < n_mid - 1:
                k1[mi][j] = km
            for jm in range(1, mi + 1):
                q1[jm - 1][j] = qm if gqb[jm] is None else qm * gqb[jm]
            q0[j] = qm if gqb[0] is None else qm * gqb[0]
            kend[j] = km if gkb is None else km * gkb
    g_all = g_mid[0]
    for mi in range(1, n_mid):
        g_all = g_all * g_mid[mi]

    def rows(pieces):
        return jnp.concatenate(pieces, axis=0)

    def slots(groups):
        return jnp.concatenate([rows(p) for p in groups], axis=1).astype(BF16)

    s1 = _dot_nt(slots(q1), slots(k1))
    s2 = _dot_nt(slots(q2), slots(k2))
    scores = (s1 + jnp.where(same_mid, s2 + jnp.where(same_sub, s3, 0.0), 0.0)).astype(BF16)

    w, vd = k3, v3
    band = jnp.sum(q3 * w, axis=-1, keepdims=True) * vd
    for d in range(1, hb):
        w = pltpu.roll(w, 1, 1) * f3
        vd = pltpu.roll(vd, 1, 1)
        sd = jnp.sum(q3 * w, axis=-1, keepdims=True)
        band = band + jnp.where(sub_h >= d, sd, 0.0) * vd
    return scores, band.reshape(C, HEAD_DIM), rows(q0).astype(BF16), rows(kend).astype(BF16), vb, g_all


def _hgrn_state(pairs, st):
    scores, band, q_st, k_end, vb, g_all = pairs
    o = _dot_nt(q_st, st.astype(BF16)) + jnp.dot(scores, vb, preferred_element_type=F32) + band
    return o, st * g_all + _dot_tn(vb, k_end)


def _hgrn_out(o, gate, ng):
    ms = jnp.mean(o * o, axis=-1, keepdims=True)
    return o * lax.rsqrt(ms + EPS) * ng * gate


def _mix_merge_kernel(a_ref, b_ref, ng_ref, ws_ref, bs_ref,
                      ga_ref, gb_ref, wa_ref, wb_ref, wo_ref, x_ref, gt_ref, *rest, tiles_per_seq, n_cast):
    cast_in, x1_ref, cast_out = rest[:n_cast], rest[n_cast], rest[n_cast + 1:2 * n_cast + 1]
    st_ref, oa_ref, ob_ref = rest[2 * n_cast + 1:]
    for src, dst in zip(cast_in, cast_out):
        dst[...] = src[...].astype(dst.dtype)

    C = HG_CHUNK
    n_heads = st_ref.shape[0]
    n_groups = ws_ref.shape[0]
    tm, D = x_ref.shape
    kw = oa_ref.shape[1]
    q_ref, lf_ref, i_ref, g_ref = (a_ref.at[:, pl.ds(k * kw, kw)] for k in range(4))
    u_ref, v_ref = (b_ref.at[:, pl.ds(k * kw, kw)] for k in range(2))

    @pl.when(lax.rem(pl.program_id(0), tiles_per_seq) == 0)
    def _():
        st_ref[...] = jnp.zeros_like(st_ref)

    r_i = lax.broadcasted_iota(jnp.int32, (C, C), 0)
    c_i = lax.broadcasted_iota(jnp.int32, (C, C), 1)
    same_mid = (r_i // HG_MID) == (c_i // HG_MID)
    same_sub = (r_i // HG_SUB) == (c_i // HG_SUB)
    sub = lax.broadcasted_iota(jnp.int32, (C // HG_SUB, HG_SUB, 1), 1)
    tril = jnp.where(r_i >= c_i, 1.0, 0.0)
    ng = ng_ref[...]
    ws = [(ws_ref[g] * tril).astype(BF16) for g in range(n_groups)]
    heads = [pl.ds(h * HEAD_DIM, HEAD_DIM) for h in range(n_heads)]
    for ci in range(tm // C):
        rows = pl.ds(ci * C, C)
        pairs = [_hgrn_pairs(q_ref[rows, sl].astype(F32), lf_ref[rows, sl].astype(F32),
                             i_ref[rows, sl], sub, same_mid, same_sub) for sl in heads]
        outs = []
        for h in range(n_heads):
            o, st_ref[h] = _hgrn_state(pairs[h], st_ref[h])
            outs.append(o)
        for sl, o in zip(heads, outs):
            oa_ref[rows, sl] = _hgrn_out(o, g_ref[rows, sl].astype(F32), ng).astype(BF16)
        for g in range(n_groups):
            cols = pl.ds(g * GROUP_DIM, GROUP_DIM)
            sg = jnp.dot(ws[g], v_ref[rows, cols], preferred_element_type=F32) + bs_ref[:, g:g + 1]
            ob_ref[rows, cols] = (u_ref[rows, cols].astype(F32) * sg).astype(BF16)

    acc = None
    for c in range(D // MERGE_TK):
        cs = pl.ds(c * MERGE_TK, MERGE_TK)
        ya = jnp.dot(oa_ref[...], wa_ref[:, cs], preferred_element_type=F32)
        yb = jnp.dot(ob_ref[...], wb_ref[:, cs], preferred_element_type=F32)
        m = jax.nn.sigmoid(ga_ref[:, cs].astype(F32)) * ya + jax.nn.sigmoid(gb_ref[:, cs].astype(F32)) * yb
        part = jnp.dot(m.astype(BF16), wo_ref[cs, :], preferred_element_type=F32)
        acc = part if acc is None else acc + part
    x1_ref[...] = x_ref[...] + gt_ref[...] * acc


def _side_castable(w, n_steps):
    return w.shape[1] % (n_steps * 16) == 0


def _mix_merge(x, p, norm_g, w_s, b_s, wa, wb, wo, l, mod, to_cast, *, seq, key_width, sgu_width):
    N, D = x.shape
    G, C, _ = w_s.shape
    assert key_width == sgu_width and C == SGU_CHUNK and G * GROUP_DIM == sgu_width and D == 2 * key_width
    kw = key_width
    tm = min(MIX_ROWS, seq)
    assert tm % HG_CHUNK == 0 and seq % tm == 0
    n_steps = N // tm
    per_b = seq // tm
    n_heads = kw // HEAD_DIM
    col = lambda c: (lambda s: (s, c))
    resident = pl.Buffered(1)
    slab = lambda w: w.shape[1] // n_steps
    cast_in_specs = [pl.BlockSpec((None, slab(w), w.shape[2]), lambda s: (l + 1, s, 0)) for w in to_cast]
    cast_out_specs = [pl.BlockSpec((slab(w), w.shape[2]), lambda s: (s, 0)) for w in to_cast]
    cast_out_shapes = [jax.ShapeDtypeStruct(w.shape[1:], BF16) for w in to_cast]
    out = pl.pallas_call(
        functools.partial(_mix_merge_kernel, tiles_per_seq=per_b, n_cast=len(to_cast)),
        grid=(n_steps,),
        in_specs=[
            pl.BlockSpec((tm, 4 * kw), col(0)),
            pl.BlockSpec((tm, 2 * kw), col(2)),
            pl.BlockSpec((1, HEAD_DIM), lambda s: (0, 0)),
            pl.BlockSpec((G, C, C), lambda s: (0, 0, 0)),
            pl.BlockSpec((C, G), lambda s: (0, 0)),
            pl.BlockSpec((tm, D), col(3)),
            pl.BlockSpec((tm, D), col(4)),
            pl.BlockSpec((None, kw, D), lambda s: (l, 0, 0), pipeline_mode=resident),
            pl.BlockSpec((None, kw, D), lambda s: (l, 0, 0), pipeline_mode=resident),
            pl.BlockSpec((None, D, D), lambda s: (l, 0, 0), pipeline_mode=resident),
            pl.BlockSpec((tm, D), col(0)),
            pl.BlockSpec((None, 1, D), lambda s: ((s // per_b) * N_MOD + 2, 0, 0)),
        ] + cast_in_specs,
        out_specs=[pl.BlockSpec((tm, D), col(0))] + cast_out_specs,
        out_shape=[jax.ShapeDtypeStruct((N, D), F32)] + cast_out_shapes,
        scratch_shapes=[
            pltpu.VMEM((n_heads, HEAD_DIM, HEAD_DIM), F32),
            pltpu.VMEM((tm, kw), BF16),
            pltpu.VMEM((tm, kw), BF16),
        ],
        compiler_params=_cparams(("arbitrary",)),
        name="mixers_merge",
    )(p, p, norm_g.reshape(1, HEAD_DIM), w_s, b_s.T, p, p, wa, wb, wo, x, mod, *to_cast)
    return out[0], list(out[1:])


def _ffn_kernel(x_ref, g_ref, sh_ref, sc_ref, gt_ref, w1_ref, w2_ref, fg_ref, o_ref, h_ref, *, final_norm):
    kk = pl.program_id(1)

    def gated_chunk(h):
        hid = jnp.dot(h, w1_ref[...], preferred_element_type=F32)
        act = jnp.square(jnp.maximum(hid, 0.0)).astype(BF16)
        return gt_ref[...] * jnp.dot(act, w2_ref[...], preferred_element_type=F32)

    @pl.when(kk == 0)
    def _():
        x = x_ref[...]
        h = _modnorm(x, g_ref[...], sh_ref[...], sc_ref[...]).astype(BF16)
        h_ref[...] = h
        o_ref[...] = x + gated_chunk(h)

    @pl.when(kk > 0)
    def _():
        o_ref[...] += gated_chunk(h_ref[...])

    if final_norm:
        @pl.when(kk == pl.num_programs(1) - 1)
        def _():
            x2 = o_ref[...]
            ms = jnp.mean(x2 * x2, axis=-1, keepdims=True)
            o_ref[...] = x2 * lax.rsqrt(ms + EPS) * fg_ref[...]


def _ffn(x1, w1, w2, mod, norm2_g, final_g, *, seq, final_norm):
    N, D = x1.shape
    FF = w1.shape[1]
    th = min(FFN_TH, FF)
    tm = min(ROW_TILE, seq)
    per_b = seq // tm
    modspec = lambda which: pl.BlockSpec((None, 1, D), lambda i, k: ((i // per_b) * N_MOD + which, 0, 0))
    return pl.pallas_call(
        functools.partial(_ffn_kernel, final_norm=final_norm),
        grid=(N // tm, FF // th),
        in_specs=[
            pl.BlockSpec((tm, D), lambda i, k: (i, 0)),
            pl.BlockSpec((1, D), lambda i, k: (0, 0)),
            modspec(3),
            modspec(4),
            modspec(5),
            pl.BlockSpec((D, th), lambda i, k: (0, k)),
            pl.BlockSpec((th, D), lambda i, k: (k, 0)),
            pl.BlockSpec((1, D), lambda i, k: (0, 0)),
        ],
        out_specs=pl.BlockSpec((tm, D), lambda i, k: (i, 0)),
        out_shape=jax.ShapeDtypeStruct((N, D), F32),
        scratch_shapes=[pltpu.VMEM((tm, D), BF16)],
        compiler_params=_cparams(("parallel", "arbitrary")),
        name="relu2_mlp",
    )(x1, norm2_g.reshape(1, D), mod, mod, mod, w1, w2, final_g.reshape(1, D))


def kernel(x, c, w_ada, b_ada, norm1_g, norm2_g, w_in, hgrn_lower_bounds, hgrn_norm_g, ln_v_g, ln_v_b,
           w_spatial, b_spatial, w_proj_a, w_proj_b, w_out, w_ff_in, w_ff_out, final_norm_g):
    B, T, D = x.shape
    L = w_ada.shape[0]
    N = B * T
    key_width = hgrn_lower_bounds.shape[1]
    sgu_width = w_proj_b.shape[1]
    assert w_proj_a.shape[1] == key_width

    mod_all = _modulation(c, w_ada, b_ada)
    lbs = _lower_bounds(hgrn_lower_bounds)
    wa_b, wb_b, wo_b = w_proj_a.astype(BF16), w_proj_b.astype(BF16), w_out.astype(BF16)
    big = (w_in, w_ff_in, w_ff_out)
    side_cast = all(_side_castable(w, N // min(MIX_ROWS, T)) for w in big)
    w_in_l, w1_l, w2_l = (w[0].astype(BF16) for w in big)

    xf = x.reshape(N, D)
    for l in range(L):
        mod = mod_all[l].reshape(B * N_MOD, 1, D)
        p = _inproj(xf, mod, norm1_g[l], w_in_l, lbs[l], ln_v_g[l], ln_v_b[l], seq=T)
        nxt = big if (side_cast and l + 1 < L) else ()
        x1, cast = _mix_merge(xf, p, hgrn_norm_g[l], w_spatial[l], b_spatial[l], wa_b, wb_b, wo_b, l, mod, nxt,
                              seq=T, key_width=key_width, sgu_width=sgu_width)
        xf = _ffn(x1, w1_l, w2_l, mod, norm2_g[l], final_norm_g, seq=T, final_norm=(l == L - 1))
        if l + 1 < L:
            w_in_l, w1_l, w2_l = cast if side_cast else (w[l + 1].astype(BF16) for w in big)
    return xf.reshape(B, T, D)
```

```python
import functools

import jax
import jax.numpy as jnp
from jax import lax
from jax.experimental import pallas as pl
from jax.experimental.pallas import tpu as pltpu

F32 = jnp.float32
BF16 = jnp.bfloat16

EPS = 1e-6
MIN_FORGET = 1e-20
N_MOD = 6
HEAD_DIM = 128
GROUP_DIM = 128
SGU_CHUNK = 128
HG_CHUNK = 128
HG_MID = 32
HG_SUB = 8
MIX_ROWS = 256
VMEM_LIMIT = 60 * 1024 * 1024
ADA_TN = 2048
ROW_TILE = 1024
INPROJ_TN = 2048
MERGE_TK = 1024
FFN_TH = 512


def _cparams(sem):
    return pltpu.CompilerParams(dimension_semantics=sem, vmem_limit_bytes=VMEM_LIMIT)


def _silu(x):
    return x * jax.nn.sigmoid(x)


def _gelu(x):
    return 0.5 * x * (1.0 + lax.erf(x * (0.5 ** 0.5)))


def _modnorm(x, g, shift, scale):
    ms = jnp.mean(x * x, axis=-1, keepdims=True)
    y = x * lax.rsqrt(ms + EPS) * g
    return y * (1.0 + scale) + shift


def _mod_kernel(c_ref, w_ref, b_ref, o_ref):
    c = c_ref[...]
    ca = _silu(c).astype(BF16)
    o_ref[...] = jnp.dot(ca, w_ref[...].astype(BF16), preferred_element_type=F32) + b_ref[...]


def _modulation(c, w_ada, b_ada):
    L, D, W = w_ada.shape
    B = c.shape[0]
    tn = min(ADA_TN, W)
    return pl.pallas_call(
        _mod_kernel,
        grid=(L, W // tn),
        in_specs=[
            pl.BlockSpec((B, D), lambda l, j: (0, 0)),
            pl.BlockSpec((None, D, tn), lambda l, j: (l, 0, j)),
            pl.BlockSpec((None, 1, tn), lambda l, j: (l, 0, j)),
        ],
        out_specs=pl.BlockSpec((None, B, tn), lambda l, j: (l, 0, j)),
        out_shape=jax.ShapeDtypeStruct((L, B, W), F32),
        compiler_params=_cparams(("arbitrary", "arbitrary")),
        name="adaln_modulation",
    )(c, w_ada, b_ada.reshape(L, 1, W))


def _lower_bounds_kernel(lb_ref, o_ref):
    x = lb_ref[...]
    m = jnp.max(x, axis=0, keepdims=True)
    e = jnp.exp(x - m)
    p = e / jnp.sum(e, axis=0, keepdims=True)
    L = x.shape[0]
    acc = jnp.zeros_like(p[0:1])
    rows = []
    for l in range(L):
        acc = acc + p[l:l + 1]
        rows.append(acc - p[0:1])
    o_ref[...] = jnp.concatenate(rows, axis=0)


def _lower_bounds(lb):
    return pl.pallas_call(
        _lower_bounds_kernel,
        out_shape=jax.ShapeDtypeStruct(lb.shape, F32),
        name="hgrn_lower_bounds",
    )(lb)


def _inproj_kernel(x_ref, sh_ref, sc_ref, g_ref, w_ref, lb_ref, lng_ref, lnb_ref, p_ref, h_ref):
    j = pl.program_id(1)
    half = w_ref.shape[1] // 2

    def project():
        return jnp.dot(h_ref[...], w_ref[...], preferred_element_type=F32)

    @pl.when(j == 0)
    def _():
        h_ref[...] = _modnorm(x_ref[...], g_ref[...], sh_ref[...], sc_ref[...]).astype(BF16)
        acc = project()
        lb = lb_ref[...]
        f = lb + (1.0 - lb) * jax.nn.sigmoid(acc[:, half:])
        p_ref[:, :half] = _silu(acc[:, :half]).astype(p_ref.dtype)
        p_ref[:, half:] = jnp.log(jnp.maximum(f, MIN_FORGET)).astype(p_ref.dtype)

    @pl.when(j == 1)
    def _():
        acc = project()
        p_ref[:, :half] = acc[:, :half].astype(p_ref.dtype)
        p_ref[:, half:] = _silu(acc[:, half:]).astype(p_ref.dtype)

    @pl.when(j == 2)
    def _():
        acc = project()
        p_ref[:, :half] = _gelu(acc[:, :half]).astype(p_ref.dtype)
        vg = _gelu(acc[:, half:])
        mu = jnp.mean(vg, axis=-1, keepdims=True)
        vc = vg - mu
        var = jnp.mean(vc * vc, axis=-1, keepdims=True)
        p_ref[:, half:] = (vc * lax.rsqrt(var + EPS) * lng_ref[...] + lnb_ref[...]).astype(p_ref.dtype)

    @pl.when(j >= 3)
    def _():
        p_ref[...] = project().astype(p_ref.dtype)


def _inproj(x, mod, norm_g, w, lb, ln_g, ln_b, *, seq):
    N, D = x.shape
    W = w.shape[1]
    tn = INPROJ_TN
    half = tn // 2
    assert lb.shape[-1] == half and ln_g.shape[-1] == half and W == 3 * tn + 2 * D and D == tn
    tm = min(ROW_TILE, seq)
    per_b = seq // tm
    return pl.pallas_call(
        _inproj_kernel,
        grid=(N // tm, W // tn),
        in_specs=[
            pl.BlockSpec((tm, D), lambda i, j: (i, 0)),
            pl.BlockSpec((None, 1, D), lambda i, j: ((i // per_b) * N_MOD + 0, 0, 0)),
            pl.BlockSpec((None, 1, D), lambda i, j: ((i // per_b) * N_MOD + 1, 0, 0)),
            pl.BlockSpec((1, D), lambda i, j: (0, 0)),
            pl.BlockSpec((D, tn), lambda i, j: (0, j)),
            pl.BlockSpec((1, half), lambda i, j: (0, 0)),
            pl.BlockSpec((1, half), lambda i, j: (0, 0)),
            pl.BlockSpec((1, half), lambda i, j: (0, 0)),
        ],
        out_specs=pl.BlockSpec((tm, tn), lambda i, j: (i, j)),
        out_shape=jax.ShapeDtypeStruct((N, W), BF16),
        scratch_shapes=[pltpu.VMEM((tm, D), BF16)],
        compiler_params=_cparams(("parallel", "arbitrary")),
        name="in_projection",
    )(x, mod, mod, norm_g.reshape(1, D), w, lb.reshape(1, half), ln_g.reshape(1, half), ln_b.reshape(1, half))


def _dot_nt(a, b):
    return lax.dot_general(a, b, (((1,), (1,)), ((), ())), preferred_element_type=F32)


def _dot_tn(a, b):
    return lax.dot_general(a, b, (((0,), (0,)), ((), ())), preferred_element_type=F32)


def _hgrn_pairs(qs, lf, vb, sub, same_mid, same_sub):
    C = HG_CHUNK
    nb = C // HG_SUB
    n_mid = C // HG_MID
    n_sub = HG_MID // HG_SUB
    hb = HG_SUB // 2
    shp3 = (nb, HG_SUB, HEAD_DIM)

    q3 = qs.reshape(shp3)
    v3 = vb.astype(F32).reshape(shp3)
    hi = sub >= hb
    sub_h = jnp.where(hi, sub - hb, sub)

    a = lf.reshape(shp3)
    f3 = jnp.exp(a)
    k3 = 1.0 - f3
    for sh in (1, 2):
        a = a + jnp.where(sub_h >= sh, pltpu.roll(a, sh, 1), 0.0)
    e4 = jnp.exp(a)
    tot = jnp.where(hi, a[:, HG_SUB - 1:HG_SUB, :], a[:, hb - 1:hb, :])
    qe4 = q3 * e4
    ke4 = k3 * jnp.exp(tot - a)
    g_lo, g_hi = e4[:, hb - 1:hb, :], e4[:, HG_SUB - 1:HG_SUB, :]
    s3 = _dot_nt(jnp.where(hi, qe4, 0.0).reshape(C, HEAD_DIM).astype(BF16),
                 jnp.where(hi, 0.0, ke4).reshape(C, HEAD_DIM).astype(BF16))
    qe = jnp.where(hi, qe4 * g_lo, qe4)
    ke = jnp.where(hi, ke4, ke4 * g_hi)
    g8 = g_lo * g_hi
    g = [g8[j] for j in range(nb)]

    def bc(r):
        return jnp.broadcast_to(r, (HG_SUB, HEAD_DIM))

    pre, suf, g_mid = [], [], []
    for mi in range(n_mid):
        gs = g[mi * n_sub:(mi + 1) * n_sub]
        p = [None]
        for i in range(1, n_sub):
            p.append(gs[i - 1] if p[-1] is None else p[-1] * gs[i - 1])
        s = [None] * n_sub
        for i in range(n_sub - 2, -1, -1):
            s[i] = gs[i + 1] if s[i + 1] is None else s[i + 1] * gs[i + 1]
        pre.append(p)
        suf.append(s)
        g_mid.append(p[-1] * gs[-1])

    zero = jnp.zeros((HG_SUB, HEAD_DIM), F32)
    q2 = [[zero] * nb for _ in range(n_sub - 1)]
    k2 = [[zero] * nb for _ in range(n_sub - 1)]
    q1 = [[zero] * nb for _ in range(n_mid - 1)]
    k1 = [[zero] * nb for _ in range(n_mid - 1)]
    q0, kend = [None] * nb, [None] * nb
    for mi in range(n_mid):
        gq = [None] * (mi + 1)
        for jm in range(mi - 1, -1, -1):
            gq[jm] = g_mid[jm] if gq[jm + 1] is None else gq[jm + 1] * g_mid[jm]
        gqb = [None if r is None else bc(r) for r in gq]
        gk = None
        for m2 in range(mi + 1, n_mid):
            gk = g_mid[m2] if gk is None else gk * g_mid[m2]
        gkb = None if gk is None else bc(gk)
        for i in range(n_sub):
            j = mi * n_sub + i
            if i < n_sub - 1:
                k2[i][j] = ke[j]
            m = qe[j]
            for jj in range(i, 0, -1):
                q2[jj - 1][j] = m
                if jj > 1:
                    m = m * bc(g[mi * n_sub + jj - 1])
            qm = qe[j] if pre[mi][i] is None else qe[j] * bc(pre[mi][i])
            km = ke[j] if suf[mi][i] is None else ke[j] * bc(suf[mi][i])
            if mi < n_mid - 1:
                k1[mi][j] = km
            for jm in range(1, mi + 1):
                q1[jm - 1][j] = qm if gqb[jm] is None else qm * gqb[jm]
            q0[j] = qm if gqb[0] is None else qm * gqb[0]
            kend[j] = km if gkb is None else km * gkb
    g_all = g_mid[0]
    for mi in range(1, n_mid):
        g_all = g_all * g_mid[mi]

    def rows(pieces):
        return jnp.concatenate(pieces, axis=0)

    def slots(groups):
        return jnp.concatenate([rows(p) for p in groups], axis=1).astype(BF16)

    s1 = _dot_nt(slots(q1), slots(k1))
    s2 = _dot_nt(slots(q2), slots(k2))
    scores = (s1 + jnp.where(same_mid, s2 + jnp.where(same_sub, s3, 0.0), 0.0)).astype(BF16)

    w, vd = k3, v3
    band = jnp.sum(q3 * w, axis=-1, keepdims=True) * vd
    for d in range(1, hb):
        w = pltpu.roll(w, 1, 1) * f3
        vd = pltpu.roll(vd, 1, 1)
        sd = jnp.sum(q3 * w, axis=-1, keepdims=True)
        band = band + jnp.where(sub_h >= d, sd, 0.0) * vd
    return scores, band.reshape(C, HEAD_DIM), rows(q0).astype(BF16), rows(kend).astype(BF16), vb, g_all


def _hgrn_state(pairs, st):
    scores, band, q_st, k_end, vb, g_all = pairs
    o = _dot_nt(q_st, st.astype(BF16)) + jnp.dot(scores, vb, preferred_element_type=F32) + band
    return o, st * g_all + _dot_tn(vb, k_end)


def _hgrn_out(o, gate, ng):
    ms = jnp.mean(o * o, axis=-1, keepdims=True)
    return o * lax.rsqrt(ms + EPS) * ng * gate


def _mix_merge_kernel(a_ref, b_ref, ng_ref, ws_ref, bs_ref,
                      ga_ref, gb_ref, wa_ref, wb_ref, wo_ref, x_ref, gt_ref, *rest, tiles_per_seq, n_cast):
    cast_in, x1_ref, cast_out = rest[:n_cast], rest[n_cast], rest[n_cast + 1:2 * n_cast + 1]
    st_ref, oa_ref, ob_ref = rest[2 * n_cast + 1:]
    for src, dst in zip(cast_in, cast_out):
        dst[...] = src[...].astype(dst.dtype)

    C = HG_CHUNK
    n_heads = st_ref.shape[0]
    n_groups = ws_ref.shape[0]
    tm, D = x_ref.shape
    kw = oa_ref.shape[1]
    q_ref, lf_ref, i_ref, g_ref = (a_ref.at[:, pl.ds(k * kw, kw)] for k in range(4))
    u_ref, v_ref = (b_ref.at[:, pl.ds(k * kw, kw)] for k in range(2))

    @pl.when(lax.rem(pl.program_id(0), tiles_per_seq) == 0)
    def _():
        st_ref[...] = jnp.zeros_like(st_ref)

    r_i = lax.broadcasted_iota(jnp.int32, (C, C), 0)
    c_i = lax.broadcasted_iota(jnp.int32, (C, C), 1)
    same_mid = (r_i // HG_MID) == (c_i // HG_MID)
    same_sub = (r_i // HG_SUB) == (c_i // HG_SUB)
    sub = lax.broadcasted_iota(jnp.int32, (C // HG_SUB, HG_SUB, 1), 1)
    tril = jnp.where(r_i >= c_i, 1.0, 0.0)
    ng = ng_ref[...]
    ws = [(ws_ref[g] * tril).astype(BF16) for g in range(n_groups)]
    heads = [pl.ds(h * HEAD_DIM, HEAD_DIM) for h in range(n_heads)]
    for ci in range(tm // C):
        rows = pl.ds(ci * C, C)
        pairs = [_hgrn_pairs(q_ref[rows, sl].astype(F32), lf_ref[rows, sl].astype(F32),
                             i_ref[rows, sl], sub, same_mid, same_sub) for sl in heads]
        outs = []
        for h in range(n_heads):
            o, st_ref[h] = _hgrn_state(pairs[h], st_ref[h])
            outs.append(o)
        for sl, o in zip(heads, outs):
            oa_ref[rows, sl] = _hgrn_out(o, g_ref[rows, sl].astype(F32), ng).astype(BF16)
        for g in range(n_groups):
            cols = pl.ds(g * GROUP_DIM, GROUP_DIM)
            sg = jnp.dot(ws[g], v_ref[rows, cols], preferred_element_type=F32) + bs_ref[:, g:g + 1]
            ob_ref[rows, cols] = (u_ref[rows, cols].astype(F32) * sg).astype(BF16)

    acc = None
    for c in range(D // MERGE_TK):
        cs = pl.ds(c * MERGE_TK, MERGE_TK)
        ya = jnp.dot(oa_ref[...], wa_ref[:, cs], preferred_element_type=F32)
        yb = jnp.dot(ob_ref[...], wb_ref[:, cs], preferred_element_type=F32)
        m = jax.nn.sigmoid(ga_ref[:, cs].astype(F32)) * ya + jax.nn.sigmoid(gb_ref[:, cs].astype(F32)) * yb
        part = jnp.dot(m.astype(BF16), wo_ref[cs, :], preferred_element_type=F32)
        acc = part if acc is None else acc + part
    x1_ref[...] = x_ref[...] + gt_ref[...] * acc


def _side_castable(w, n_steps):
    return w.shape[1] % (n_steps * 16) == 0


def _mix_merge(x, p, norm_g, w_s, b_s, wa, wb, wo, l, mod, to_cast, *, seq, key_width, sgu_width):
    N, D = x.shape
    G, C, _ = w_s.shape
    assert key_width == sgu_width and C == SGU_CHUNK and G * GROUP_DIM == sgu_width and D == 2 * key_width
    kw = key_width
    tm = min(MIX_ROWS, seq)
    assert tm % HG_CHUNK == 0 and seq % tm == 0
    n_steps = N // tm
    per_b = seq // tm
    n_heads = kw // HEAD_DIM
    col = lambda c: (lambda s: (s, c))
    resident = pl.Buffered(1)
    slab = lambda w: w.shape[1] // n_steps
    cast_in_specs = [pl.BlockSpec((None, slab(w), w.shape[2]), lambda s: (l + 1, s, 0)) for w in to_cast]
    cast_out_specs = [pl.BlockSpec((slab(w), w.shape[2]), lambda s: (s, 0)) for w in to_cast]
    cast_out_shapes = [jax.ShapeDtypeStruct(w.shape[1:], BF16) for w in to_cast]
    out = pl.pallas_call(
        functools.partial(_mix_merge_kernel, tiles_per_seq=per_b, n_cast=len(to_cast)),
        grid=(n_steps,),
        in_specs=[
            pl.BlockSpec((tm, 4 * kw), col(0)),
            pl.BlockSpec((tm, 2 * kw), col(2)),
            pl.BlockSpec((1, HEAD_DIM), lambda s: (0, 0)),
            pl.BlockSpec((G, C, C), lambda s: (0, 0, 0)),
            pl.BlockSpec((C, G), lambda s: (0, 0)),
            pl.BlockSpec((tm, D), col(3)),
            pl.BlockSpec((tm, D), col(4)),
            pl.BlockSpec((None, kw, D), lambda s: (l, 0, 0), pipeline_mode=resident),
            pl.BlockSpec((None, kw, D), lambda s: (l, 0, 0), pipeline_mode=resident),
            pl.BlockSpec((None, D, D), lambda s: (l, 0, 0), pipeline_mode=resident),
            pl.BlockSpec((tm, D), col(0)),
            pl.BlockSpec((None, 1, D), lambda s: ((s // per_b) * N_MOD + 2, 0, 0)),
        ] + cast_in_specs,
        out_specs=[pl.BlockSpec((tm, D), col(0))] + cast_out_specs,
        out_shape=[jax.ShapeDtypeStruct((N, D), F32)] + cast_out_shapes,
        scratch_shapes=[
            pltpu.VMEM((n_heads, HEAD_DIM, HEAD_DIM), F32),
            pltpu.VMEM((tm, kw), BF16),
            pltpu.VMEM((tm, kw), BF16),
        ],
        compiler_params=_cparams(("arbitrary",)),
        name="mixers_merge",
    )(p, p, norm_g.reshape(1, HEAD_DIM), w_s, b_s.T, p, p, wa, wb, wo, x, mod, *to_cast)
    return out[0], list(out[1:])


def _ffn_kernel(x_ref, g_ref, sh_ref, sc_ref, gt_ref, w1_hbm, w2_hbm, fg_ref, o_ref, h_ref, w1_buf, w2_buf, sem,
                *, final_norm):
    i, n_i = pl.program_id(0), pl.num_programs(0)
    th = w1_buf.shape[2]
    n_k = w1_hbm.shape[1] // th

    def copies(k, slot):
        lo = pl.multiple_of(k * th, th)
        return (pltpu.make_async_copy(w1_hbm.at[:, pl.ds(lo, th)], w1_buf.at[slot], sem.at[slot, 0]),
                pltpu.make_async_copy(w2_hbm.at[pl.ds(lo, th), :], w2_buf.at[slot], sem.at[slot, 1]))

    def start(k, slot):
        for c in copies(k, slot):
            c.start()

    @pl.when(i == 0)
    def _():
        start(0, 0)

    def gated_chunk(h, slot):
        hid = jnp.dot(h, w1_buf[slot], preferred_element_type=F32)
        act = jnp.square(jnp.maximum(hid, 0.0)).astype(BF16)
        return gt_ref[...] * jnp.dot(act, w2_buf[slot], preferred_element_type=F32)

    for c in copies(0, 0):
        c.wait()
    start(1, 1)
    x = x_ref[...]
    h = _modnorm(x, g_ref[...], sh_ref[...], sc_ref[...]).astype(BF16)
    h_ref[...] = h
    o_ref[...] = x + gated_chunk(h, 0)

    def chunk(k, carry):
        slot = lax.rem(k, 2)
        for c in copies(k, slot):
            c.wait()

        @pl.when(k + 1 < n_k)
        def _():
            start(k + 1, 1 - slot)

        @pl.when((k + 1 == n_k) & (i + 1 < n_i))
        def _():
            start(0, 0)

        o_ref[...] += gated_chunk(h_ref[...], slot)
        return carry

    lax.fori_loop(1, n_k, chunk, 0)

    if final_norm:
        x2 = o_ref[...]
        ms = jnp.mean(x2 * x2, axis=-1, keepdims=True)
        o_ref[...] = x2 * lax.rsqrt(ms + EPS) * fg_ref[...]


def _ffn(x1, w1, w2, mod, norm2_g, final_g, *, seq, final_norm):
    N, D = x1.shape
    FF = w1.shape[1]
    th = min(FFN_TH, FF)
    tm = min(ROW_TILE, seq)
    per_b = seq // tm
    assert FF % th == 0 and (FF // th) % 2 == 0
    modspec = lambda which: pl.BlockSpec((None, 1, D), lambda i: ((i // per_b) * N_MOD + which, 0, 0))
    return pl.pallas_call(
        functools.partial(_ffn_kernel, final_norm=final_norm),
        grid=(N // tm,),
        in_specs=[
            pl.BlockSpec((tm, D), lambda i: (i, 0)),
            pl.BlockSpec((1, D), lambda i: (0, 0)),
            modspec(3),
            modspec(4),
            modspec(5),
            pl.BlockSpec(memory_space=pl.ANY),
            pl.BlockSpec(memory_space=pl.ANY),
            pl.BlockSpec((1, D), lambda i: (0, 0)),
        ],
        out_specs=pl.BlockSpec((tm, D), lambda i: (i, 0)),
        out_shape=jax.ShapeDtypeStruct((N, D), F32),
        scratch_shapes=[
            pltpu.VMEM((tm, D), BF16),
            pltpu.VMEM((2, D, th), BF16),
            pltpu.VMEM((2, th, D), BF16),
            pltpu.SemaphoreType.DMA((2, 2)),
        ],
        compiler_params=_cparams(("arbitrary",)),
        name="relu2_mlp",
    )(x1, norm2_g.reshape(1, D), mod, mod, mod, w1, w2, final_g.reshape(1, D))


def kernel(x, c, w_ada, b_ada, norm1_g, norm2_g, w_in, hgrn_lower_bounds, hgrn_norm_g, ln_v_g, ln_v_b,
           w_spatial, b_spatial, w_proj_a, w_proj_b, w_out, w_ff_in, w_ff_out, final_norm_g):
    B, T, D = x.shape
    L = w_ada.shape[0]
    N = B * T
    key_width = hgrn_lower_bounds.shape[1]
    sgu_width = w_proj_b.shape[1]
    assert w_proj_a.shape[1] == key_width

    mod_all = _modulation(c, w_ada, b_ada)
    lbs = _lower_bounds(hgrn_lower_bounds)
    wa_b, wb_b, wo_b = w_proj_a.astype(BF16), w_proj_b.astype(BF16), w_out.astype(BF16)
    big = (w_in, w_ff_in, w_ff_out)
    side_cast = all(_side_castable(w, N // min(MIX_ROWS, T)) for w in big)
    w_in_l, w1_l, w2_l = (w[0].astype(BF16) for w in big)

    xf = x.reshape(N, D)
    for l in range(L):
        mod = mod_all[l].reshape(B * N_MOD, 1, D)
        p = _inproj(xf, mod, norm1_g[l], w_in_l, lbs[l], ln_v_g[l], ln_v_b[l], seq=T)
        nxt = big if (side_cast and l + 1 < L) else ()
        x1, cast = _mix_merge(xf, p, hgrn_norm_g[l], w_spatial[l], b_spatial[l], wa_b, wb_b, wo_b, l, mod, nxt,
                              seq=T, key_width=key_width, sgu_width=sgu_width)
        xf = _ffn(x1, w1_l, w2_l, mod, norm2_g[l], final_norm_g, seq=T, final_norm=(l == L - 1))
        if l + 1 < L:
            w_in_l, w1_l, w2_l = cast if side_cast else (w[l + 1].astype(BF16) for w in big)
    return xf.reshape(B, T, D)
```

```python
import functools

import jax
import jax.numpy as jnp
from jax import lax
from jax.experimental import pallas as pl
from jax.experimental.pallas import tpu as pltpu

F32 = jnp.float32
BF16 = jnp.bfloat16

EPS = 1e-6
MIN_FORGET = 1e-20
N_MOD = 6
HEAD_DIM = 128
GROUP_DIM = 128
SGU_CHUNK = 128
HG_CHUNK = 128
HG_MID = 32
HG_SUB = 8
MIX_ROWS = 256
VMEM_LIMIT = 60 * 1024 * 1024
ADA_TN = 2048
ROW_TILE = 1024
INPROJ_TN = 2048
MERGE_TK = 1024
FFN_TH = 1024


def _cparams(sem):
    return pltpu.CompilerParams(dimension_semantics=sem, vmem_limit_bytes=VMEM_LIMIT)


def _silu(x):
    return x * jax.nn.sigmoid(x)


def _gelu(x):
    return 0.5 * x * (1.0 + lax.erf(x * (0.5 ** 0.5)))


def _modnorm(x, g, shift, scale):
    ms = jnp.mean(x * x, axis=-1, keepdims=True)
    y = x * lax.rsqrt(ms + EPS) * g
    return y * (1.0 + scale) + shift


def _mod_kernel(c_ref, w_ref, b_ref, o_ref):
    c = c_ref[...]
    ca = _silu(c).astype(BF16)
    o_ref[...] = jnp.dot(ca, w_ref[...].astype(BF16), preferred_element_type=F32) + b_ref[...]


def _modulation(c, w_ada, b_ada):
    L, D, W = w_ada.shape
    B = c.shape[0]
    tn = min(ADA_TN, W)
    return pl.pallas_call(
        _mod_kernel,
        grid=(L, W // tn),
        in_specs=[
            pl.BlockSpec((B, D), lambda l, j: (0, 0)),
            pl.BlockSpec((None, D, tn), lambda l, j: (l, 0, j)),
            pl.BlockSpec((None, 1, tn), lambda l, j: (l, 0, j)),
        ],
        out_specs=pl.BlockSpec((None, B, tn), lambda l, j: (l, 0, j)),
        out_shape=jax.ShapeDtypeStruct((L, B, W), F32),
        compiler_params=_cparams(("arbitrary", "arbitrary")),
        name="adaln_modulation",
    )(c, w_ada, b_ada.reshape(L, 1, W))


def _lower_bounds_kernel(lb_ref, o_ref):
    x = lb_ref[...]
    m = jnp.max(x, axis=0, keepdims=True)
    e = jnp.exp(x - m)
    p = e / jnp.sum(e, axis=0, keepdims=True)
    L = x.shape[0]
    acc = jnp.zeros_like(p[0:1])
    rows = []
    for l in range(L):
        acc = acc + p[l:l + 1]
        rows.append(acc - p[0:1])
    o_ref[...] = jnp.concatenate(rows, axis=0)


def _lower_bounds(lb):
    return pl.pallas_call(
        _lower_bounds_kernel,
        out_shape=jax.ShapeDtypeStruct(lb.shape, F32),
        name="hgrn_lower_bounds",
    )(lb)


def _inproj_kernel(x_ref, sh_ref, sc_ref, g_ref, w_ref, lb_ref, lng_ref, lnb_ref, p_ref, h_ref):
    j, r = pl.program_id(1), pl.program_id(2)
    half = w_ref.shape[1] // 2

    def project():
        return jnp.dot(h_ref[r], w_ref[...], preferred_element_type=F32)

    @pl.when(j == 0)
    def _():
        h_ref[r] = _modnorm(x_ref[...], g_ref[...], sh_ref[...], sc_ref[...]).astype(BF16)
        acc = project()
        lb = lb_ref[...]
        f = lb + (1.0 - lb) * jax.nn.sigmoid(acc[:, half:])
        p_ref[:, :half] = _silu(acc[:, :half]).astype(p_ref.dtype)
        p_ref[:, half:] = jnp.log(jnp.maximum(f, MIN_FORGET)).astype(p_ref.dtype)

    @pl.when(j == 1)
    def _():
        acc = project()
        p_ref[:, :half] = acc[:, :half].astype(p_ref.dtype)
        p_ref[:, half:] = _silu(acc[:, half:]).astype(p_ref.dtype)

    @pl.when(j == 2)
    def _():
        acc = project()
        p_ref[:, :half] = _gelu(acc[:, :half]).astype(p_ref.dtype)
        vg = _gelu(acc[:, half:])
        mu = jnp.mean(vg, axis=-1, keepdims=True)
        vc = vg - mu
        var = jnp.mean(vc * vc, axis=-1, keepdims=True)
        p_ref[:, half:] = (vc * lax.rsqrt(var + EPS) * lng_ref[...] + lnb_ref[...]).astype(p_ref.dtype)

    @pl.when(j >= 3)
    def _():
        p_ref[...] = project().astype(p_ref.dtype)


def _inproj(x, mod, norm_g, w, lb, ln_g, ln_b, *, seq):
    N, D = x.shape
    W = w.shape[1]
    tn = INPROJ_TN
    half = tn // 2
    assert lb.shape[-1] == half and ln_g.shape[-1] == half and W == 3 * tn + 2 * D and D == tn
    tm = min(ROW_TILE, seq)
    per_b = seq // tm
    pair = 2 if (N // tm) % 2 == 0 else 1

    def xtile(i, j, r):
        return jnp.where(j == 0, pair * i + r, pair * i + pair - 1)

    return pl.pallas_call(
        _inproj_kernel,
        grid=(N // tm // pair, W // tn, pair),
        in_specs=[
            pl.BlockSpec((tm, D), lambda i, j, r: (xtile(i, j, r), 0)),
            pl.BlockSpec((None, 1, D), lambda i, j, r: ((xtile(i, j, r) // per_b) * N_MOD + 0, 0, 0)),
            pl.BlockSpec((None, 1, D), lambda i, j, r: ((xtile(i, j, r) // per_b) * N_MOD + 1, 0, 0)),
            pl.BlockSpec((1, D), lambda i, j, r: (0, 0)),
            pl.BlockSpec((D, tn), lambda i, j, r: (0, j)),
            pl.BlockSpec((1, half), lambda i, j, r: (0, 0)),
            pl.BlockSpec((1, half), lambda i, j, r: (0, 0)),
            pl.BlockSpec((1, half), lambda i, j, r: (0, 0)),
        ],
        out_specs=pl.BlockSpec((tm, tn), lambda i, j, r: (pair * i + r, j)),
        out_shape=jax.ShapeDtypeStruct((N, W), BF16),
        scratch_shapes=[pltpu.VMEM((pair, tm, D), BF16)],
        compiler_params=_cparams(("parallel", "arbitrary", "arbitrary")),
        name="in_projection",
    )(x, mod, mod, norm_g.reshape(1, D), w, lb.reshape(1, half), ln_g.reshape(1, half), ln_b.reshape(1, half))


def _dot_nt(a, b):
    return lax.dot_general(a, b, (((1,), (1,)), ((), ())), preferred_element_type=F32)


def _dot_tn(a, b):
    return lax.dot_general(a, b, (((0,), (0,)), ((), ())), preferred_element_type=F32)


def _hgrn_pairs(qs, lf, vb, sub, same_mid, same_sub):
    C = HG_CHUNK
    nb = C // HG_SUB
    n_mid = C // HG_MID
    n_sub = HG_MID // HG_SUB
    hb = HG_SUB // 2
    shp3 = (nb, HG_SUB, HEAD_DIM)

    q3 = qs.reshape(shp3)
    v3 = vb.astype(F32).reshape(shp3)
    hi = sub >= hb
    sub_h = jnp.where(hi, sub - hb, sub)

    a = lf.reshape(shp3)
    f3 = jnp.exp(a)
    k3 = 1.0 - f3
    for sh in (1, 2):
        a = a + jnp.where(sub_h >= sh, pltpu.roll(a, sh, 1), 0.0)
    e4 = jnp.exp(a)
    tot = jnp.where(hi, a[:, HG_SUB - 1:HG_SUB, :], a[:, hb - 1:hb, :])
    qe4 = q3 * e4
    ke4 = k3 * jnp.exp(tot - a)
    g_lo, g_hi = e4[:, hb - 1:hb, :], e4[:, HG_SUB - 1:HG_SUB, :]
    s3 = _dot_nt(jnp.where(hi, qe4, 0.0).reshape(C, HEAD_DIM).astype(BF16),
                 jnp.where(hi, 0.0, ke4).reshape(C, HEAD_DIM).astype(BF16))
    qe = jnp.where(hi, qe4 * g_lo, qe4)
    ke = jnp.where(hi, ke4, ke4 * g_hi)
    g8 = g_lo * g_hi
    g = [g8[j] for j in range(nb)]

    def bc(r):
        return jnp.broadcast_to(r, (HG_SUB, HEAD_DIM))

    pre, suf, g_mid = [], [], []
    for mi in range(n_mid):
        gs = g[mi * n_sub:(mi + 1) * n_sub]
        p = [None]
        for i in range(1, n_sub):
            p.append(gs[i - 1] if p[-1] is None else p[-1] * gs[i - 1])
        s = [None] * n_sub
        for i in range(n_sub - 2, -1, -1):
            s[i] = gs[i + 1] if s[i + 1] is None else s[i + 1] * gs[i + 1]
        pre.append(p)
        suf.append(s)
        g_mid.append(p[-1] * gs[-1])

    zero = jnp.zeros((HG_SUB, HEAD_DIM), F32)
    q2 = [[zero] * nb for _ in range(n_sub - 1)]
    k2 = [[zero] * nb for _ in range(n_sub - 1)]
    q1 = [[zero] * nb for _ in range(n_mid - 1)]
    k1 = [[zero] * nb for _ in range(n_mid - 1)]
    q0, kend = [None] * nb, [None] * nb
    for mi in range(n_mid):
        gq = [None] * (mi + 1)
        for jm in range(mi - 1, -1, -1):
            gq[jm] = g_mid[jm] if gq[jm + 1] is None else gq[jm + 1] * g_mid[jm]
        gqb = [None if r is None else bc(r) for r in gq]
        gk = None
        for m2 in range(mi + 1, n_mid):
            gk = g_mid[m2] if gk is None else gk * g_mid[m2]
        gkb = None if gk is None else bc(gk)
        for i in range(n_sub):
            j = mi * n_sub + i
            if i < n_sub - 1:
                k2[i][j] = ke[j]
            m = qe[j]
            for jj in range(i, 0, -1):
                q2[jj - 1][j] = m
                if jj > 1:
                    m = m * bc(g[mi * n_sub + jj - 1])
            qm = qe[j] if pre[mi][i] is None else qe[j] * bc(pre[mi][i])
            km = ke[j] if suf[mi][i] is None else ke[j] * bc(suf[mi][i])
            if mi < n_mid - 1:
                k1[mi][j] = km
            for jm in range(1, mi + 1):
                q1[jm - 1][j] = qm if gqb[jm] is None else qm * gqb[jm]
            q0[j] = qm if gqb[0] is None else qm * gqb[0]
            kend[j] = km if gkb is None else km * gkb
    g_all = g_mid[0]
    for mi in range(1, n_mid):
        g_all = g_all * g_mid[mi]

    def rows(pieces):
        return jnp.concatenate(pieces, axis=0)

    def slots(groups):
        return jnp.concatenate([rows(p) for p in groups], axis=1).astype(BF16)

    s1 = _dot_nt(slots(q1), slots(k1))
    s2 = _dot_nt(slots(q2), slots(k2))
    scores = (s1 + jnp.where(same_mid, s2 + jnp.where(same_sub, s3, 0.0), 0.0)).astype(BF16)

    w, vd = k3, v3
    band = jnp.sum(q3 * w, axis=-1, keepdims=True) * vd
    for d in range(1, hb):
        w = pltpu.roll(w, 1, 1) * f3
        vd = pltpu.roll(vd, 1, 1)
        sd = jnp.sum(q3 * w, axis=-1, keepdims=True)
        band = band + jnp.where(sub_h >= d, sd, 0.0) * vd
    return scores, band.reshape(C, HEAD_DIM), rows(q0).astype(BF16), rows(kend).astype(BF16), vb, g_all


def _hgrn_state(pairs, st):
    scores, band, q_st, k_end, vb, g_all = pairs
    o = _dot_nt(q_st, st.astype(BF16)) + jnp.dot(scores, vb, preferred_element_type=F32) + band
    return o, st * g_all + _dot_tn(vb, k_end)


def _hgrn_out(o, gate, ng):
    ms = jnp.mean(o * o, axis=-1, keepdims=True)
    return o * lax.rsqrt(ms + EPS) * ng * gate


def _mix_merge_kernel(a_ref, b_ref, ng_ref, ws_ref, bs_ref,
                      ga_ref, gb_ref, wa_ref, wb_ref, wo_ref, x_ref, gt_ref, *rest, tiles_per_seq, n_cast):
    cast_in, x1_ref, cast_out = rest[:n_cast], rest[n_cast], rest[n_cast + 1:2 * n_cast + 1]
    st_ref, oa_ref, ob_ref = rest[2 * n_cast + 1:]
    for src, dst in zip(cast_in, cast_out):
        dst[...] = src[...].astype(dst.dtype)

    C = HG_CHUNK
    n_heads = st_ref.shape[0]
    n_groups = ws_ref.shape[0]
    tm, D = x_ref.shape
    kw = oa_ref.shape[1]
    q_ref, lf_ref, i_ref, g_ref = (a_ref.at[:, pl.ds(k * kw, kw)] for k in range(4))
    u_ref, v_ref = (b_ref.at[:, pl.ds(k * kw, kw)] for k in range(2))

    @pl.when(lax.rem(pl.program_id(0), tiles_per_seq) == 0)
    def _():
        st_ref[...] = jnp.zeros_like(st_ref)

    r_i = lax.broadcasted_iota(jnp.int32, (C, C), 0)
    c_i = lax.broadcasted_iota(jnp.int32, (C, C), 1)
    same_mid = (r_i // HG_MID) == (c_i // HG_MID)
    same_sub = (r_i // HG_SUB) == (c_i // HG_SUB)
    sub = lax.broadcasted_iota(jnp.int32, (C // HG_SUB, HG_SUB, 1), 1)
    tril = jnp.where(r_i >= c_i, 1.0, 0.0)
    ng = ng_ref[...]
    ws = [(ws_ref[g] * tril).astype(BF16) for g in range(n_groups)]
    heads = [pl.ds(h * HEAD_DIM, HEAD_DIM) for h in range(n_heads)]
    for ci in range(tm // C):
        rows = pl.ds(ci * C, C)
        pairs = [_hgrn_pairs(q_ref[rows, sl].astype(F32), lf_ref[rows, sl].astype(F32),
                             i_ref[rows, sl], sub, same_mid, same_sub) for sl in heads]
        outs = []
        for h in range(n_heads):
            o, st_ref[h] = _hgrn_state(pairs[h], st_ref[h])
            outs.append(o)
        for sl, o in zip(heads, outs):
            oa_ref[rows, sl] = _hgrn_out(o, g_ref[rows, sl].astype(F32), ng).astype(BF16)
        for g in range(n_groups):
            cols = pl.ds(g * GROUP_DIM, GROUP_DIM)
            sg = jnp.dot(ws[g], v_ref[rows, cols], preferred_element_type=F32) + bs_ref[:, g:g + 1]
            ob_ref[rows, cols] = (u_ref[rows, cols].astype(F32) * sg).astype(BF16)

    acc = None
    for c in range(D // MERGE_TK):
        cs = pl.ds(c * MERGE_TK, MERGE_TK)
        ya = jnp.dot(oa_ref[...], wa_ref[:, cs], preferred_element_type=F32)
        yb = jnp.dot(ob_ref[...], wb_ref[:, cs], preferred_element_type=F32)
        m = jax.nn.sigmoid(ga_ref[:, cs].astype(F32)) * ya + jax.nn.sigmoid(gb_ref[:, cs].astype(F32)) * yb
        part = jnp.dot(m.astype(BF16), wo_ref[cs, :], preferred_element_type=F32)
        acc = part if acc is None else acc + part
    x1_ref[...] = x_ref[...] + gt_ref[...] * acc


def _side_castable(w, n_steps):
    return w.shape[1] % (n_steps * 16) == 0


def _mix_merge(x, p, norm_g, w_s, b_s, wa, wb, wo, l, mod, to_cast, *, seq, key_width, sgu_width):
    N, D = x.shape
    G, C, _ = w_s.shape
    assert key_width == sgu_width and C == SGU_CHUNK and G * GROUP_DIM == sgu_width and D == 2 * key_width
    kw = key_width
    tm = min(MIX_ROWS, seq)
    assert tm % HG_CHUNK == 0 and seq % tm == 0
    n_steps = N // tm
    per_b = seq // tm
    n_heads = kw // HEAD_DIM
    col = lambda c: (lambda s: (s, c))
    resident = pl.Buffered(1)
    slab = lambda w: w.shape[1] // n_steps
    cast_in_specs = [pl.BlockSpec((None, slab(w), w.shape[2]), lambda s: (l + 1, s, 0)) for w in to_cast]
    cast_out_specs = [pl.BlockSpec((slab(w), w.shape[2]), lambda s: (s, 0)) for w in to_cast]
    cast_out_shapes = [jax.ShapeDtypeStruct(w.shape[1:], BF16) for w in to_cast]
    out = pl.pallas_call(
        functools.partial(_mix_merge_kernel, tiles_per_seq=per_b, n_cast=len(to_cast)),
        grid=(n_steps,),
        in_specs=[
            pl.BlockSpec((tm, 4 * kw), col(0)),
            pl.BlockSpec((tm, 2 * kw), col(2)),
            pl.BlockSpec((1, HEAD_DIM), lambda s: (0, 0)),
            pl.BlockSpec((G, C, C), lambda s: (0, 0, 0)),
            pl.BlockSpec((C, G), lambda s: (0, 0)),
            pl.BlockSpec((tm, D), col(3)),
            pl.BlockSpec((tm, D), col(4)),
            pl.BlockSpec((None, kw, D), lambda s: (l, 0, 0), pipeline_mode=resident),
            pl.BlockSpec((None, kw, D), lambda s: (l, 0, 0), pipeline_mode=resident),
            pl.BlockSpec((None, D, D), lambda s: (l, 0, 0), pipeline_mode=resident),
            pl.BlockSpec((tm, D), col(0)),
            pl.BlockSpec((None, 1, D), lambda s: ((s // per_b) * N_MOD + 2, 0, 0)),
        ] + cast_in_specs,
        out_specs=[pl.BlockSpec((tm, D), col(0))] + cast_out_specs,
        out_shape=[jax.ShapeDtypeStruct((N, D), F32)] + cast_out_shapes,
        scratch_shapes=[
            pltpu.VMEM((n_heads, HEAD_DIM, HEAD_DIM), F32),
            pltpu.VMEM((tm, kw), BF16),
            pltpu.VMEM((tm, kw), BF16),
        ],
        compiler_params=_cparams(("arbitrary",)),
        name="mixers_merge",
    )(p, p, norm_g.reshape(1, HEAD_DIM), w_s, b_s.T, p, p, wa, wb, wo, x, mod, *to_cast)
    return out[0], list(out[1:])


def _ffn_kernel(x_ref, g_ref, sh_ref, sc_ref, gt_ref, w1_ref, w2_ref, fg_ref, o_ref, h_ref, *, final_norm):
    kk = pl.program_id(1)

    def gated_chunk(h):
        hid = jnp.dot(h, w1_ref[...], preferred_element_type=F32)
        act = jnp.square(jnp.maximum(hid, 0.0)).astype(BF16)
        return gt_ref[...] * jnp.dot(act, w2_ref[...], preferred_element_type=F32)

    @pl.when(kk == 0)
    def _():
        x = x_ref[...]
        h = _modnorm(x, g_ref[...], sh_ref[...], sc_ref[...]).astype(BF16)
        h_ref[...] = h
        o_ref[...] = x + gated_chunk(h)

    @pl.when(kk > 0)
    def _():
        o_ref[...] += gated_chunk(h_ref[...])

    if final_norm:
        @pl.when(kk == pl.num_programs(1) - 1)
        def _():
            x2 = o_ref[...]
            ms = jnp.mean(x2 * x2, axis=-1, keepdims=True)
            o_ref[...] = x2 * lax.rsqrt(ms + EPS) * fg_ref[...]


def _ffn(x1, w1, w2, mod, norm2_g, final_g, *, seq, final_norm):
    N, D = x1.shape
    FF = w1.shape[1]
    th = min(FFN_TH, FF)
    tm = min(ROW_TILE, seq)
    per_b = seq // tm
    modspec = lambda which: pl.BlockSpec((None, 1, D), lambda i, k: ((i // per_b) * N_MOD + which, 0, 0))
    return pl.pallas_call(
        functools.partial(_ffn_kernel, final_norm=final_norm),
        grid=(N // tm, FF // th),
        in_specs=[
            pl.BlockSpec((tm, D), lambda i, k: (i, 0)),
            pl.BlockSpec((1, D), lambda i, k: (0, 0)),
            modspec(3),
            modspec(4),
            modspec(5),
            pl.BlockSpec((D, th), lambda i, k: (0, k)),
            pl.BlockSpec((th, D), lambda i, k: (k, 0)),
            pl.BlockSpec((1, D), lambda i, k: (0, 0)),
        ],
        out_specs=pl.BlockSpec((tm, D), lambda i, k: (i, 0)),
        out_shape=jax.ShapeDtypeStruct((N, D), F32),
        scratch_shapes=[pltpu.VMEM((tm, D), BF16)],
        compiler_params=_cparams(("parallel", "arbitrary")),
        name="relu2_mlp",
    )(x1, norm2_g.reshape(1, D), mod, mod, mod, w1, w2, final_g.reshape(1, D))


def kernel(x, c, w_ada, b_ada, norm1_g, norm2_g, w_in, hgrn_lower_bounds, hgrn_norm_g, ln_v_g, ln_v_b,
           w_spatial, b_spatial, w_proj_a, w_proj_b, w_out, w_ff_in, w_ff_out, final_norm_g):
    B, T, D = x.shape
    L = w_ada.shape[0]
    N = B * T
    key_width = hgrn_lower_bounds.shape[1]
    sgu_width = w_proj_b.shape[1]
    assert w_proj_a.shape[1] == key_width

    mod_all = _modulation(c, w_ada, b_ada)
    lbs = _lower_bounds(hgrn_lower_bounds)
    wa_b, wb_b, wo_b = w_proj_a.astype(BF16), w_proj_b.astype(BF16), w_out.astype(BF16)
    big = (w_in, w_ff_in, w_ff_out)
    side_cast = all(_side_castable(w, N // min(MIX_ROWS, T)) for w in big)
    w_in_l, w1_l, w2_l = (w[0].astype(BF16) for w in big)

    xf = x.reshape(N, D)
    for l in range(L):
        mod = mod_all[l].reshape(B * N_MOD, 1, D)
        p = _inproj(xf, mod, norm1_g[l], w_in_l, lbs[l], ln_v_g[l], ln_v_b[l], seq=T)
        nxt = big if (side_cast and l + 1 < L) else ()
        x1, cast = _mix_merge(xf, p, hgrn_norm_g[l], w_spatial[l], b_spatial[l], wa_b, wb_b, wo_b, l, mod, nxt,
                              seq=T, key_width=key_width, sgu_width=sgu_width)
        xf = _ffn(x1, w1_l, w2_l, mod, norm2_g[l], final_norm_g, seq=T, final_norm=(l == L - 1))
        if l + 1 < L:
            w_in_l, w1_l, w2_l = cast if side_cast else (w[l + 1].astype(BF16) for w in big)
    return xf.reshape(B, T, D)
```
